```python
import math
import jax, jax.numpy as jnp
from jax import lax
import numpy as np

D_MODEL = 1024
BATCH = 2
SEQ = 8192
DEPTH = 1
DEC_BATCH = 32
DEC_SEQ = 1
PAST_LEN = 16384
PAGE_SIZE = 128

N_MEM = 256
SB_HEADS = 8
SB_HEAD_DIM = 64
SB_WIDTH = SB_HEADS * SB_HEAD_DIM
SB_BIAS_INIT = -6.0
CONV_WIDTH = D_MODEL // 2
CONV_K = 3
XA_HEADS = 4
XA_HEAD_DIM = D_MODEL // 8
XA_WIDTH = XA_HEADS * XA_HEAD_DIM
D_FF = 4 * D_MODEL
Q_BLOCK = 128
N_BRANCH = 3
RMS_EPS = 1e-6
IN_COLS = 3 * SB_WIDTH + 3 * CONV_WIDTH + XA_WIDTH
IN_SPLITS = tuple(np.cumsum([SB_WIDTH, SB_WIDTH, SB_WIDTH, CONV_WIDTH, CONV_WIDTH, CONV_WIDTH])[:].tolist())

kernel_name = 'stickbreak_shortconv_memxattn_hybrid_step'


def rms_norm(x, g):
    xf = x.astype(jnp.float32)
    y = xf * lax.rsqrt(jnp.mean(xf * xf, axis=-1, keepdims=True) + RMS_EPS)
    return (y * g.astype(jnp.float32)).astype(x.dtype)


def split_in(u):
    q, k, v, cb, cc, ch, xq = jnp.split(u, IN_SPLITS, axis=-1)
    b, t = u.shape[0], u.shape[1]
    hd = lambda a, h, d: a.reshape(b, t, h, d)
    return (hd(q, SB_HEADS, SB_HEAD_DIM), hd(k, SB_HEADS, SB_HEAD_DIM), hd(v, SB_HEADS, SB_HEAD_DIM),
            cb, cc, ch, hd(xq, XA_HEADS, XA_HEAD_DIM))


def sb_block(q, q_pos, k, v, k_pos, b_sb):
    z = jnp.einsum('bqhd,bkhd->bhqk', q.astype(jnp.float32), k.astype(jnp.float32)) * (1.0 / math.sqrt(SB_HEAD_DIM))
    z = z + b_sb.astype(jnp.float32)[None, :, None, None]
    mask = k_pos[None, :] < q_pos[:, None]
    log_keep = jnp.where(mask, jax.nn.log_sigmoid(-z), 0.0)
    after = lax.cumsum(log_keep, axis=3, reverse=True) - log_keep
    w = jnp.where(mask, jnp.exp(jax.nn.log_sigmoid(z) + after), 0.0)
    return jnp.einsum('bhqk,bkhd->bqhd', w, v.astype(jnp.float32)).astype(v.dtype)


def sb_prompt(q, k, v, b_sb):
    b, s, h, d = q.shape
    nb = s // Q_BLOCK
    qb = q.reshape(b, nb, Q_BLOCK, h, d).transpose(1, 0, 2, 3, 4)
    pos = jnp.arange(s)
    qp = pos.reshape(nb, Q_BLOCK)
    out = lax.map(lambda a: sb_block(a[0], a[1], k, v, pos, b_sb), (qb, qp))
    return out.transpose(1, 0, 2, 3, 4).reshape(b, s, h * d)


def causal_conv(u_ext, w):
    t = u_ext.shape[1] - (CONV_K - 1)
    return sum(u_ext[:, i:i + t] * w[i] for i in range(CONV_K))


def mem_kv(mem, g_mem, w_mem_kv):
    kv = rms_norm(mem, g_mem) @ w_mem_kv
    b, m = mem.shape[0], mem.shape[1]
    mk, mv = jnp.split(kv, 2, axis=-1)
    return mk.reshape(b, m, XA_HEADS, XA_HEAD_DIM), mv.reshape(b, m, XA_HEADS, XA_HEAD_DIM)


def cross_attn(q, mk, mv):
    s = jnp.einsum('bqhd,bmhd->bhqm', q.astype(jnp.float32), mk.astype(jnp.float32)) * (1.0 / math.sqrt(XA_HEAD_DIM))
    p = jax.nn.softmax(s, axis=-1)
    o = jnp.einsum('bhqm,bmhd->bqhd', p, mv.astype(jnp.float32)).astype(mv.dtype)
    return o.reshape(q.shape[0], q.shape[1], XA_WIDTH)


def merge_and_ffn(x, xn, y_sb, y_conv, y_xa, w_gate, b_gate, w_sb_o, w_conv_o, w_xa_o, w_o,
                  g_mix_post, g_ffn_pre, w_up, w_down, g_ffn_post):
    g = jax.nn.sigmoid((xn @ w_gate + b_gate).astype(jnp.float32)).astype(x.dtype)
    g = g.reshape(x.shape[0], x.shape[1], N_BRANCH, D_MODEL)
    m = g[:, :, 0] * (y_sb @ w_sb_o) + g[:, :, 1] * (y_conv @ w_conv_o) + g[:, :, 2] * (y_xa @ w_xa_o)
    h = x + rms_norm(m @ w_o, g_mix_post)
    f = jnp.square(jax.nn.relu(rms_norm(h, g_ffn_pre) @ w_up)) @ w_down
    return h + rms_norm(f, g_ffn_post)


def setup_inputs(seed: int = 0) -> dict:
    key = jax.random.key(seed)
    ks = jax.random.split(key, 32)
    n_pages = PAST_LEN // PAGE_SIZE
    n_phys = (DEC_BATCH * n_pages * 5) // 4
    nrm = lambda k, shp, s: jax.random.normal(k, shp, jnp.float32) * s
    gain = lambda k: 1.0 + nrm(k, (DEPTH, D_MODEL), 0.05)
    page_table = jax.random.permutation(ks[7], n_phys)[:DEC_BATCH * n_pages].reshape(DEC_BATCH, n_pages).astype(jnp.int32)
    return {
        'x_prompt': nrm(ks[0], (BATCH, SEQ, D_MODEL), 1.0),
        'x_sample': nrm(ks[1], (DEC_BATCH, DEC_SEQ, D_MODEL), 1.0),
        'mem_prompt': nrm(ks[2], (BATCH, N_MEM, D_MODEL), 1.0),
        'cache_k_pages': nrm(ks[3], (DEPTH, n_phys, PAGE_SIZE, SB_HEADS, SB_HEAD_DIM), 1.0),
        'cache_v_pages': nrm(ks[4], (DEPTH, n_phys, PAGE_SIZE, SB_HEADS, SB_HEAD_DIM), 1.0),
        'page_table': page_table,
        'cache_mem_k': nrm(ks[5], (DEPTH, DEC_BATCH, N_MEM, XA_HEADS, XA_HEAD_DIM), 1.0),
        'cache_mem_v': nrm(ks[6], (DEPTH, DEC_BATCH, N_MEM, XA_HEADS, XA_HEAD_DIM), 1.0),
        'state_conv': nrm(ks[8], (DEPTH, DEC_BATCH, CONV_K - 1, CONV_WIDTH), 1.0),
        'g_mix_pre': gain(ks[9]),
        'w_in': nrm(ks[10], (DEPTH, D_MODEL, IN_COLS), D_MODEL ** -0.5),
        'b_sb': SB_BIAS_INIT + nrm(ks[25], (DEPTH, SB_HEADS), 0.1),
        'w_conv': nrm(ks[11], (DEPTH, CONV_K, CONV_WIDTH), CONV_K ** -0.5),
        'g_mem': gain(ks[12]),
        'w_mem_kv': nrm(ks[13], (DEPTH, D_MODEL, 2 * XA_WIDTH), D_MODEL ** -0.5),
        'w_gate': nrm(ks[14], (DEPTH, D_MODEL, N_BRANCH * D_MODEL), D_MODEL ** -0.5),
        'b_gate': nrm(ks[15], (DEPTH, N_BRANCH * D_MODEL), 0.1),
        'w_sb_o': nrm(ks[16], (DEPTH, SB_WIDTH, D_MODEL), SB_WIDTH ** -0.5),
        'w_conv_o': nrm(ks[17], (DEPTH, CONV_WIDTH, D_MODEL), CONV_WIDTH ** -0.5),
        'w_xa_o': nrm(ks[18], (DEPTH, XA_WIDTH, D_MODEL), XA_WIDTH ** -0.5),
        'w_o': nrm(ks[19], (DEPTH, D_MODEL, D_MODEL), D_MODEL ** -0.5),
        'g_mix_post': gain(ks[20]),
        'g_ffn_pre': gain(ks[21]),
        'w_up': nrm(ks[22], (DEPTH, D_MODEL, D_FF), D_MODEL ** -0.5),
        'w_down': nrm(ks[23], (DEPTH, D_FF, D_MODEL), D_FF ** -0.5),
        'g_ffn_post': gain(ks[24]),
    }


def reference(x_prompt, x_sample, mem_prompt, cache_k_pages, cache_v_pages, page_table,
              cache_mem_k, cache_mem_v, state_conv, g_mix_pre, w_in, b_sb, w_conv, g_mem, w_mem_kv,
              w_gate, b_gate, w_sb_o, w_conv_o, w_xa_o, w_o, g_mix_post, g_ffn_pre, w_up, w_down,
              g_ffn_post):
    n_pages = PAST_LEN // PAGE_SIZE
    xp, xs = x_prompt, x_sample
    kp_l, vp_l, cp_l, mkp_l, mvp_l, ks_l, vs_l, cs_l = [], [], [], [], [], [], [], []
    for l in range(DEPTH):
        tail = (w_gate[l], b_gate[l], w_sb_o[l], w_conv_o[l], w_xa_o[l], w_o[l],
                g_mix_post[l], g_ffn_pre[l], w_up[l], w_down[l], g_ffn_post[l])
        xn = rms_norm(xp, g_mix_pre[l])
        q, k, v, cb, cc, ch, xq = split_in(xn @ w_in[l])
        y_sb = sb_prompt(q, k, v, b_sb[l])
        c_ext = jnp.pad(cc * ch, ((0, 0), (CONV_K - 1, 0), (0, 0)))
        y_conv = cb * causal_conv(c_ext, w_conv[l])
        mk, mv = mem_kv(mem_prompt, g_mem[l], w_mem_kv[l])
        y_xa = cross_attn(xq, mk, mv)
        kp_l.append(k); vp_l.append(v); cp_l.append(c_ext[:, -(CONV_K - 1):])
        mkp_l.append(mk); mvp_l.append(mv)
        xp = merge_and_ffn(xp, xn, y_sb, y_conv, y_xa, *tail)
        xn = rms_norm(xs, g_mix_pre[l])
        q, k, v, cb, cc, ch, xq = split_in(xn @ w_in[l])
        k_past = cache_k_pages[l][page_table].reshape(DEC_BATCH, n_pages * PAGE_SIZE, SB_HEADS, SB_HEAD_DIM)
        v_past = cache_v_pages[l][page_table].reshape(DEC_BATCH, n_pages * PAGE_SIZE, SB_HEADS, SB_HEAD_DIM)
        k_all = jnp.concatenate([k_past, k], axis=1)
        v_all = jnp.concatenate([v_past, v], axis=1)
        k_pos = jnp.arange(PAST_LEN + DEC_SEQ)
        q_pos = PAST_LEN + jnp.arange(DEC_SEQ)
        y_sb = sb_block(q, q_pos, k_all, v_all, k_pos, b_sb[l]).reshape(DEC_BATCH, DEC_SEQ, SB_WIDTH)
        c_ext = jnp.concatenate([state_conv[l].astype(cc.dtype), cc * ch], axis=1)
        y_conv = cb * causal_conv(c_ext, w_conv[l])
        y_xa = cross_attn(xq, cache_mem_k[l], cache_mem_v[l])
        ks_l.append(k); vs_l.append(v); cs_l.append(c_ext[:, -(CONV_K - 1):])
        xs = merge_and_ffn(xs, xn, y_sb, y_conv, y_xa, *tail)
    return (xp, xs, jnp.stack(kp_l), jnp.stack(vp_l), jnp.stack(cp_l), jnp.stack(mkp_l), jnp.stack(mvp_l),
            jnp.stack(ks_l), jnp.stack(vs_l), jnp.stack(cs_l))
```

```python
import functools
import math

import jax
import jax.numpy as jnp
from jax import lax
from jax.experimental import pallas as pl
from jax.experimental.pallas import tpu as pltpu

F32 = jnp.float32
BF16 = jnp.bfloat16

D_MODEL = 1024
SB_HEADS = 8
SB_HEAD_DIM = 64
SB_WIDTH = SB_HEADS * SB_HEAD_DIM
CONV_WIDTH = 512
CONV_K = 3
XA_HEADS = 4
XA_HEAD_DIM = 128
XA_WIDTH = XA_HEADS * XA_HEAD_DIM
N_MEM = 256
D_FF = 4 * D_MODEL
N_BRANCH = 3
RMS_EPS = 1e-6
PAGE_SIZE = 128
IN_CHUNK = 512
SB_SCALE = 1.0 / math.sqrt(SB_HEAD_DIM)
XA_SCALE = 1.0 / math.sqrt(XA_HEAD_DIM)

V7X_VMEM_BYTES = 64 * 1024 * 1024
VMEM_LIMIT = V7X_VMEM_BYTES - 8 * 1024 * 1024

PROMPT_TILE = 512
SB_TILE = 256
DECODE_PAGES = 8
FF_CHUNK = 512


def _params(*sem):
    return pltpu.CompilerParams(dimension_semantics=sem, vmem_limit_bytes=VMEM_LIMIT)


def _resident(shape):
    nd = len(shape)
    return pl.BlockSpec(shape, lambda *_: (0,) * nd, pipeline_mode=pl.Buffered(1))


def _rms(x, g):
    return x * lax.rsqrt(jnp.mean(x * x, axis=-1, keepdims=True) + RMS_EPS) * g


def _dot(a, b):
    return jnp.dot(a, b, preferred_element_type=F32)


def _dot_nt(a, b):
    return lax.dot_general(a, b, (((1,), (1,)), ((), ())), preferred_element_type=F32)


def _softplus(z):
    return jnp.maximum(z, 0.0) + jnp.log(1.0 + jnp.exp(-jnp.abs(z)))


def _neg_upper(n):
    s = lax.broadcasted_iota(jnp.int32, (n, n), 0)
    j = lax.broadcasted_iota(jnp.int32, (n, n), 1)
    return jnp.where(s > j, -1.0, 0.0).astype(BF16)


def _memkv_kernel(mem_ref, g_ref, w_ref, kv_ref, kvb_ref):
    mn = _rms(mem_ref[...], g_ref[...]).astype(BF16)
    kv = _dot(mn, w_ref[...])
    kv_ref[...] = kv
    kvb_ref[...] = kv.astype(BF16)


def _memkv(mem2d, g_mem, w_mem_kv_b):
    rows = mem2d.shape[0]
    return pl.pallas_call(
        _memkv_kernel,
        out_shape=(jax.ShapeDtypeStruct((rows, 2 * XA_WIDTH), F32),
                   jax.ShapeDtypeStruct((rows, 2 * XA_WIDTH), BF16)),
        compiler_params=pltpu.CompilerParams(vmem_limit_bytes=VMEM_LIMIT),
        name="memkv",
    )(mem2d, g_mem, w_mem_kv_b)


def _proj_prompt_kernel(x_ref, g_ref, win_ref, wconv_ref, mk_ref, mv_ref,
                        q_ref, k_ref, v_ref, kt_ref, vt_ref, yconv_ref, cstate_ref, yxa_ref,
                        hist_ref):
    tm = x_ref.shape[1]
    i = pl.program_id(1)
    xn = _rms(x_ref[0], g_ref[...]).astype(BF16)

    def proj(c):
        return _dot(xn, win_ref[:, c * IN_CHUNK:(c + 1) * IN_CHUNK])

    q = proj(0) * SB_SCALE
    k = proj(1)
    v = proj(2)
    for h in range(SB_HEADS):
        sl = slice(h * SB_HEAD_DIM, (h + 1) * SB_HEAD_DIM)
        q_ref[0, h] = q[:, sl].astype(BF16)
        k_ref[0, h] = k[:, sl].astype(BF16)
        v_ref[0, h] = v[:, sl].astype(BF16)
    kt_ref[0] = k.T
    vt_ref[0] = v.T

    cb = proj(3)
    c = proj(4) * proj(5)

    @pl.when(i == 0)
    def _():
        hist_ref[0:8, :] = jnp.zeros((8, CONV_WIDTH), F32)

    hist_ref[8:8 + tm, :] = c
    c_m1 = hist_ref[pl.ds(7, tm), :]
    c_m2 = hist_ref[pl.ds(6, tm), :]
    w = wconv_ref[...]
    yconv_ref[0] = (cb * (c_m2 * w[0:1] + c_m1 * w[1:2] + c * w[2:3])).astype(BF16)
    tail = c[tm - (CONV_K - 1):tm, :]
    hist_ref[6:8, :] = tail
    cstate_ref[0] = tail

    xq = proj(6) * XA_SCALE
    for h in range(XA_HEADS):
        sl = slice(h * XA_HEAD_DIM, (h + 1) * XA_HEAD_DIM)
        s = _dot_nt(xq[:, sl].astype(BF16), mk_ref[0, :, sl])
        e = jnp.exp(s - jnp.max(s, axis=-1, keepdims=True))
        o = _dot(e.astype(BF16), mv_ref[0, :, sl]) / jnp.sum(e, axis=-1, keepdims=True)
        yxa_ref[0, :, sl] = o.astype(BF16)


def _proj_prompt(x, g, w_in_b, w_conv, mk_b, mv_b):
    nb, seq, _ = x.shape
    tm = PROMPT_TILE
    head_spec = pl.BlockSpec((1, SB_HEADS, tm, SB_HEAD_DIM), lambda b, i: (b, 0, i, 0))
    t_spec = pl.BlockSpec((1, SB_WIDTH, tm), lambda b, i: (b, 0, i))
    row_spec = lambda w: pl.BlockSpec((1, tm, w), lambda b, i: (b, i, 0))
    mem_spec = pl.BlockSpec((1, N_MEM, XA_WIDTH), lambda b, i: (b, 0, 0))
    head_shape = jax.ShapeDtypeStruct((nb, SB_HEADS, seq, SB_HEAD_DIM), BF16)
    t_shape = jax.ShapeDtypeStruct((nb, SB_WIDTH, seq), F32)
    return pl.pallas_call(
        _proj_prompt_kernel,
        grid=(nb, seq // tm),
        in_specs=[row_spec(D_MODEL), _resident(g.shape), _resident(w_in_b.shape),
                  _resident(w_conv.shape), mem_spec, mem_spec],
        out_specs=(head_spec, head_spec, head_spec, t_spec, t_spec, row_spec(CONV_WIDTH),
                   pl.BlockSpec((1, CONV_K - 1, CONV_WIDTH), lambda b, i: (b, 0, 0)),
                   row_spec(XA_WIDTH)),
        out_shape=(head_shape, head_shape, head_shape, t_shape, t_shape,
                   jax.ShapeDtypeStruct((nb, seq, CONV_WIDTH), BF16),
                   jax.ShapeDtypeStruct((nb, CONV_K - 1, CONV_WIDTH), F32),
                   jax.ShapeDtypeStruct((nb, seq, XA_WIDTH), BF16)),
        scratch_shapes=[pltpu.VMEM((tm + 8, CONV_WIDTH), F32)],
        compiler_params=_params("arbitrary", "arbitrary"),
        name="proj_prompt",
    )(x, g, w_in_b, w_conv, mk_b, mv_b)


def _sb_prompt_kernel(bias_ref, q_ref, k_ref, v_ref, negu_ref, o_ref):
    t = q_ref.shape[2]
    h = pl.program_id(1)
    i = pl.program_id(2)
    bias = bias_ref[h]
    q = q_ref[0, 0]
    negu = negu_ref[...]
    row = lax.broadcasted_iota(jnp.int32, (t, t), 0)
    col = lax.broadcasted_iota(jnp.int32, (t, t), 1)
    causal = col < row

    def block(j, acc, carry, diagonal):
        start = pl.multiple_of(j * t, t)
        kj = k_ref[0, 0, pl.ds(start, t), :]
        vj = v_ref[0, 0, pl.ds(start, t), :]
        z = _dot_nt(q, kj) + bias
        p = _softplus(z)
        if diagonal:
            p = jnp.where(causal, p, 0.0)
        after = _dot(p.astype(BF16), negu)
        w = jnp.exp((z - p) + (after + carry))
        if diagonal:
            w = jnp.where(causal, w, 0.0)
        acc = acc + _dot(w.astype(BF16), vj)
        carry = carry + (after[:, 0:1] - p[:, 0:1])
        return acc, carry

    acc0 = jnp.zeros((t, SB_HEAD_DIM), F32)
    carry0 = jnp.zeros((t, 1), F32)
    state = block(i, acc0, carry0, True)
    acc, _ = lax.fori_loop(0, i, lambda n, st: block(i - 1 - n, st[0], st[1], False), state)
    o_ref[0, 0] = acc.astype(BF16)


def _sb_prompt(b_sb, q, k, v):
    nb, nh, seq, dh = q.shape
    t = SB_TILE
    kv_spec = pl.BlockSpec((1, 1, seq, dh), lambda b, h, i: (b, h, 0, 0))
    qo_spec = pl.BlockSpec((1, 1, t, dh), lambda b, h, i: (b, h, i, 0))
    return pl.pallas_call(
        _sb_prompt_kernel,
        grid=(nb, nh, seq // t),
        in_specs=[pl.BlockSpec(memory_space=pltpu.SMEM), qo_spec, kv_spec, kv_spec,
                  _resident((t, t))],
        out_specs=qo_spec,
        out_shape=jax.ShapeDtypeStruct((nb, nh, seq, dh), BF16),
        compiler_params=_params("arbitrary", "arbitrary", "arbitrary"),
        name="sb_prompt",
    )(b_sb, q, k, v, _neg_upper(t))


def _merge_kernel(x_ref, ysb_ref, yconv_ref, yxa_ref, gpre_ref, wgate_ref, bgate_ref,
                  wsb_ref, wcv_ref, wxa_ref, wo_ref, gpost_ref, h_ref):
    x = x_ref[0]
    xn = _rms(x, gpre_ref[...]).astype(BF16)

    def gate(br):
        sl = slice(br * D_MODEL, (br + 1) * D_MODEL)
        return 1.0 / (1.0 + jnp.exp(-(_dot(xn, wgate_ref[:, sl]) + bgate_ref[:, sl])))

    y_sb = _dot(ysb_ref[0, 0], wsb_ref[0:SB_HEAD_DIM, :])
    for h in range(1, SB_HEADS):
        y_sb = y_sb + _dot(ysb_ref[0, h], wsb_ref[h * SB_HEAD_DIM:(h + 1) * SB_HEAD_DIM, :])
    m = gate(0) * y_sb
    m = m + gate(1) * _dot(yconv_ref[0], wcv_ref[...])
    m = m + gate(2) * _dot(yxa_ref[0], wxa_ref[...])
    mo = _dot(m.astype(BF16), wo_ref[...])
    h_ref[0] = x + _rms(mo, gpost_ref[...])


def _merge(x, ysb, yconv, yxa, g_pre, w_gate_b, b_gate, w_sb_b, w_cv_b, w_xa_b, w_o_b, g_post, tm):
    nb, seq, _ = x.shape
    row_spec = lambda w: pl.BlockSpec((1, tm, w), lambda b, i: (b, i, 0))
    consts = (g_pre, w_gate_b, b_gate, w_sb_b, w_cv_b, w_xa_b, w_o_b, g_post)
    return pl.pallas_call(
        _merge_kernel,
        grid=(nb, seq // tm),
        in_specs=[row_spec(D_MODEL),
                  pl.BlockSpec((1, SB_HEADS, tm, SB_HEAD_DIM), lambda b, i: (b, 0, i, 0)),
                  row_spec(CONV_WIDTH), row_spec(XA_WIDTH)] + [_resident(c.shape) for c in consts],
        out_specs=row_spec(D_MODEL),
        out_shape=jax.ShapeDtypeStruct((nb, seq, D_MODEL), F32),
        compiler_params=_params("arbitrary", "arbitrary"),
        name="merge",
    )(x, ysb, yconv, yxa, *consts)


def _ffn_kernel(h_ref, gpre_ref, wup_ref, wdown_ref, gpost_ref, o_ref):
    h = h_ref[0]
    hn = _rms(h, gpre_ref[...]).astype(BF16)
    f = jnp.zeros(h.shape, F32)
    for c in range(D_FF // FF_CHUNK):
        sl = slice(c * FF_CHUNK, (c + 1) * FF_CHUNK)
        a = jnp.maximum(_dot(hn, wup_ref[:, sl]), 0.0)
        f = f + _dot((a * a).astype(BF16), wdown_ref[sl, :])
    o_ref[0] = h + _rms(f, gpost_ref[...])


def _ffn(h, g_pre, w_up_b, w_down_b, g_post, tm):
    nb, seq, _ = h.shape
    row_spec = pl.BlockSpec((1, tm, D_MODEL), lambda b, i: (b, i, 0))
    consts = (g_pre, w_up_b, w_down_b, g_post)
    return pl.pallas_call(
        _ffn_kernel,
        grid=(nb, seq // tm),
        in_specs=[row_spec] + [_resident(c.shape) for c in consts],
        out_specs=row_spec,
        out_shape=jax.ShapeDtypeStruct((nb, seq, D_MODEL), F32),
        compiler_params=_params("arbitrary", "arbitrary"),
        name="ffn",
    )(h, *consts)


def _proj_sample_kernel(x_ref, g_ref, win_ref, wconv_ref, s0_ref, s1_ref,
                        q_ref, k_ref, v_ref, yconv_ref, c_ref, xq_ref):
    xn = _rms(x_ref[...], g_ref[...]).astype(BF16)

    def proj(c):
        return _dot(xn, win_ref[:, c * IN_CHUNK:(c + 1) * IN_CHUNK])

    q_ref[...] = proj(0) * SB_SCALE
    k_ref[...] = proj(1)
    v_ref[...] = proj(2)
    cb = proj(3)
    c = proj(4) * proj(5)
    w = wconv_ref[...]
    yconv_ref[...] = (cb * (s0_ref[...] * w[0:1] + s1_ref[...] * w[1:2] + c * w[2:3])).astype(BF16)
    c_ref[...] = c
    xq_ref[...] = proj(6) * XA_SCALE


def _proj_sample(x2d, g, w_in_b, w_conv, s0, s1):
    n = x2d.shape[0]
    wide = jax.ShapeDtypeStruct((n, IN_CHUNK), F32)
    return pl.pallas_call(
        _proj_sample_kernel,
        out_shape=(wide, wide, wide, jax.ShapeDtypeStruct((n, CONV_WIDTH), BF16), wide, wide),
        compiler_params=pltpu.CompilerParams(vmem_limit_bytes=VMEM_LIMIT),
        name="proj_sample",
    )(x2d, g, w_in_b, w_conv, s0, s1)


def _xattn_sample_kernel(xq_ref, mk_ref, mv_ref, o_ref):
    xq = xq_ref[0]
    for h in range(XA_HEADS):
        sl = slice(h * XA_HEAD_DIM, (h + 1) * XA_HEAD_DIM)
        qh = jnp.broadcast_to(xq[:, sl], (8, XA_HEAD_DIM)).astype(BF16)
        s = _dot_nt(qh, mk_ref[0, :, sl].astype(BF16))
        e = jnp.exp(s - jnp.max(s, axis=-1, keepdims=True))
        o = _dot(e.astype(BF16), mv_ref[0, :, sl].astype(BF16)) / jnp.sum(e, axis=-1, keepdims=True)
        o_ref[0, :, sl] = o[0:1].astype(BF16)


def _xattn_sample(xq, mk, mv):
    n = xq.shape[0]
    q_spec = pl.BlockSpec((1, 1, XA_WIDTH), lambda b: (b, 0, 0))
    m_spec = pl.BlockSpec((1, N_MEM, XA_WIDTH), lambda b: (b, 0, 0))
    return pl.pallas_call(
        _xattn_sample_kernel,
        grid=(n,),
        in_specs=[q_spec, m_spec, m_spec],
        out_specs=q_spec,
        out_shape=jax.ShapeDtypeStruct((n, 1, XA_WIDTH), BF16),
        compiler_params=_params("arbitrary"),
        name="xattn_sample",
    )(xq.reshape(n, 1, XA_WIDTH), mk, mv)


def _sb_decode_kernel(pt_ref, q_ref, bias_ref, negu_ref, *refs):
    del pt_ref
    n = DECODE_PAGES
    k_refs, v_refs = refs[:n], refs[n:2 * n]
    o_ref, acc_ref, carry_ref = refs[2 * n:]
    g = pl.program_id(1)

    @pl.when(g == 0)
    def _():
        acc_ref[...] = jnp.zeros(acc_ref.shape, F32)
        carry_ref[...] = jnp.zeros(carry_ref.shape, F32)

    qb = q_ref[0]
    z = jnp.concatenate([jnp.sum(qb * k_refs[p][0], axis=1) for p in range(n)], axis=0)
    z = z + bias_ref[...]
    p_all = _softplus(z)
    negu = negu_ref[...]
    p_hi = p_all.astype(BF16)
    rest = p_all - p_hi.astype(F32)
    p_mid = rest.astype(BF16)
    p_lo = (rest - p_mid.astype(F32)).astype(BF16)
    after = _dot(p_hi, negu) + _dot(p_mid, negu) + _dot(p_lo, negu)
    total = after[:, 0:1] - p_all[:, 0:1]
    carry = carry_ref[...]
    ws = []
    for p in range(n):
        sl = slice(p * SB_HEADS, (p + 1) * SB_HEADS)
        ws.append(jnp.exp((z[sl] - p_all[sl]) + (after[sl] + carry)))
        carry = carry + total[sl]
    carry_ref[...] = carry
    for h in range(SB_HEADS):
        acc = acc_ref[h]
        for p in range(n):
            acc = acc + ws[p][h:h + 1, :] * v_refs[p][0, h]
        acc_ref[h] = acc

    @pl.when(g == pl.num_programs(1) - 1)
    def _():
        o_ref[0] = jnp.sum(acc_ref[...], axis=2, keepdims=True)


def _sb_decode(page_table, q, b_sb, kt_pages, vt_pages):
    nseq, n_pages = page_table.shape
    n = DECODE_PAGES
    bias = jnp.tile(b_sb.reshape(SB_HEADS, 1), (n, 1))

    def page_spec(p):
        return pl.BlockSpec(
            (1, SB_HEADS, SB_HEAD_DIM, PAGE_SIZE),
            lambda b, g, pt: (pt[b, n_pages - 1 - (g * n + p)], 0, 0, 0))

    q_spec = pl.BlockSpec((1, SB_HEADS, SB_HEAD_DIM, 1), lambda b, g, pt: (b, 0, 0, 0))
    grid_spec = pltpu.PrefetchScalarGridSpec(
        num_scalar_prefetch=1,
        grid=(nseq, n_pages // n),
        in_specs=[q_spec,
                  pl.BlockSpec((n * SB_HEADS, 1), lambda b, g, pt: (0, 0)),
                  pl.BlockSpec((PAGE_SIZE, PAGE_SIZE), lambda b, g, pt: (0, 0))]
                 + [page_spec(p) for p in range(n)] * 2,
        out_specs=q_spec,
        scratch_shapes=[pltpu.VMEM((SB_HEADS, SB_HEAD_DIM, PAGE_SIZE), F32),
                        pltpu.VMEM((SB_HEADS, 1), F32)],
    )
    return pl.pallas_call(
        _sb_decode_kernel,
        grid_spec=grid_spec,
        out_shape=jax.ShapeDtypeStruct((nseq, SB_HEADS, SB_HEAD_DIM, 1), F32),
        compiler_params=_params("arbitrary", "arbitrary"),
        name="sb_decode",
    )(page_table, q.reshape(nseq, SB_HEADS, SB_HEAD_DIM, 1), bias, _neg_upper(PAGE_SIZE),
      *([kt_pages] * n), *([vt_pages] * n))


def kernel(x_prompt, x_sample, mem_prompt, cache_k_pages, cache_v_pages, page_table, cache_mem_k,
           cache_mem_v, state_conv, g_mix_pre, w_in, b_sb, w_conv, g_mem, w_mem_kv, w_gate, b_gate,
           w_sb_o, w_conv_o, w_xa_o, w_o, g_mix_post, g_ffn_pre, w_up, w_down, g_ffn_post):
    depth = w_in.shape[0]
    assert depth == 1, "single-layer step"
    nb, seq, _ = x_prompt.shape
    nseq = x_sample.shape[0]
    bf = lambda a: a[0].astype(BF16)
    w_in_b, w_gate_b, w_mem_b = bf(w_in), bf(w_gate), bf(w_mem_kv)
    w_sb_b, w_cv_b, w_xa_b, w_o_b = bf(w_sb_o), bf(w_conv_o), bf(w_xa_o), bf(w_o)
    w_up_b, w_down_b = bf(w_up), bf(w_down)
    tail = (g_mix_pre, w_gate_b, b_gate, w_sb_b, w_cv_b, w_xa_b, w_o_b, g_mix_post)
    ffn_w = (g_ffn_pre, w_up_b, w_down_b, g_ffn_post)

    kv, kv_b = _memkv(mem_prompt.reshape(nb * N_MEM, D_MODEL), g_mem, w_mem_b)
    mk = kv[:, :XA_WIDTH].reshape(1, nb, N_MEM, XA_HEADS, XA_HEAD_DIM)
    mv = kv[:, XA_WIDTH:].reshape(1, nb, N_MEM, XA_HEADS, XA_HEAD_DIM)
    mk_b = kv_b[:, :XA_WIDTH].reshape(nb, N_MEM, XA_WIDTH)
    mv_b = kv_b[:, XA_WIDTH:].reshape(nb, N_MEM, XA_WIDTH)
    q_p, k_p, v_p, kt, vt, yconv_p, cstate_p, yxa_p = _proj_prompt(
        x_prompt, g_mix_pre, w_in_b, w_conv[0], mk_b, mv_b)
    ysb_p = _sb_prompt(b_sb[0], q_p, k_p, v_p)
    h_p = _merge(x_prompt, ysb_p, yconv_p, yxa_p, *tail, tm=PROMPT_TILE)
    y_p = _ffn(h_p, *ffn_w, tm=PROMPT_TILE)
    to_cache = lambda a: jnp.transpose(
        a.reshape(nb, SB_HEADS, SB_HEAD_DIM, seq), (0, 3, 1, 2))[None]
    k_prompt, v_prompt = to_cache(kt), to_cache(vt)

    xs = x_sample.reshape(nseq, D_MODEL)
    q_s, k_s, v_s, yconv_s, c_s, xq_s = _proj_sample(
        xs, g_mix_pre, w_in_b, w_conv[0], state_conv[0, :, 0, :], state_conv[0, :, 1, :])
    kt_pages = jnp.transpose(cache_k_pages[0], (0, 2, 3, 1))
    vt_pages = jnp.transpose(cache_v_pages[0], (0, 2, 3, 1))
    ysb_s = _sb_decode(page_table, q_s, b_sb[0], kt_pages, vt_pages)
    ysb_s = jnp.transpose(ysb_s.reshape(nseq, SB_HEADS, SB_HEAD_DIM), (1, 0, 2))[None].astype(BF16)
    yxa_s = _xattn_sample(xq_s, cache_mem_k[0].reshape(nseq, N_MEM, XA_WIDTH),
                          cache_mem_v[0].reshape(nseq, N_MEM, XA_WIDTH))
    h_s = _merge(xs[None], ysb_s, yconv_s[None], yxa_s.reshape(1, nseq, XA_WIDTH), *tail, tm=nseq)
    y_s = _ffn(h_s, *ffn_w, tm=nseq)
    conv_sample = jnp.stack([state_conv[0, :, 1, :], c_s], axis=1)[None]
    to_tok = lambda a: a.reshape(1, nseq, 1, SB_HEADS, SB_HEAD_DIM)

    return (y_p, y_s.reshape(nseq, 1, D_MODEL), k_prompt, v_prompt, cstate_p[None], mk, mv,
            to_tok(k_s), to_tok(v_s), conv_sample)
```

```python
import functools
import math

import jax
import jax.numpy as jnp
from jax import lax
from jax.experimental import pallas as pl
from jax.experimental.pallas import tpu as pltpu

F32 = jnp.float32
BF16 = jnp.bfloat16

D_MODEL = 1024
SB_HEADS = 8
SB_HEAD_DIM = 64
SB_WIDTH = SB_HEADS * SB_HEAD_DIM
CONV_WIDTH = 512
CONV_K = 3
XA_HEADS = 4
XA_HEAD_DIM = 128
XA_WIDTH = XA_HEADS * XA_HEAD_DIM
N_MEM = 256
D_FF = 4 * D_MODEL
N_BRANCH = 3
RMS_EPS = 1e-6
PAGE_SIZE = 128
IN_CHUNK = 512
LOG2E = math.log2(math.e)
SB_QSCALE = LOG2E / math.sqrt(SB_HEAD_DIM)
MASKED_LOG2 = -1e30
XA_SCALE = 1.0 / math.sqrt(XA_HEAD_DIM)

V7X_VMEM_BYTES = 64 * 1024 * 1024
VMEM_LIMIT = V7X_VMEM_BYTES - 8 * 1024 * 1024

PROMPT_TILE = 512
SB_TILE = 256
SB_HEAD_GROUP = 4
DECODE_PAGES = 8
FF_CHUNK = 512


def _params(*sem):
    return pltpu.CompilerParams(dimension_semantics=sem, vmem_limit_bytes=VMEM_LIMIT)


def _resident(shape):
    nd = len(shape)
    return pl.BlockSpec(shape, lambda *_: (0,) * nd, pipeline_mode=pl.Buffered(1))


def _rms(x, g):
    return x * lax.rsqrt(jnp.mean(x * x, axis=-1, keepdims=True) + RMS_EPS) * g


def _dot(a, b):
    return jnp.dot(a, b, preferred_element_type=F32)


def _dot_nt(a, b):
    return lax.dot_general(a, b, (((1,), (1,)), ((), ())), preferred_element_type=F32)


def _softplus2(z2):
    return jnp.maximum(z2, jnp.log2(1.0 + jnp.exp2(jnp.minimum(z2, 64.0))))


def _neg_upper(n):
    s = lax.broadcasted_iota(jnp.int32, (n, n), 0)
    j = lax.broadcasted_iota(jnp.int32, (n, n), 1)
    return jnp.where(s > j, -1.0, 0.0).astype(BF16)


def _memkv_kernel(mem_ref, g_ref, w_ref, kv_ref, kvb_ref):
    mn = _rms(mem_ref[...], g_ref[...]).astype(BF16)
    kv = _dot(mn, w_ref[...])
    kv_ref[...] = kv
    kvb_ref[...] = kv.astype(BF16)


def _memkv(mem2d, g_mem, w_mem_kv_b):
    rows = mem2d.shape[0]
    return pl.pallas_call(
        _memkv_kernel,
        out_shape=(jax.ShapeDtypeStruct((rows, 2 * XA_WIDTH), F32),
                   jax.ShapeDtypeStruct((rows, 2 * XA_WIDTH), BF16)),
        compiler_params=pltpu.CompilerParams(vmem_limit_bytes=VMEM_LIMIT),
        name="memkv",
    )(mem2d, g_mem, w_mem_kv_b)


def _proj_prompt_kernel(x_ref, g_ref, win_ref, wconv_ref, mk_ref, mv_ref,
                        q_ref, k_ref, v_ref, kt_ref, vt_ref, yconv_ref, cstate_ref, yxa_ref,
                        hist_ref):
    tm = x_ref.shape[1]
    i = pl.program_id(1)
    xn = _rms(x_ref[0], g_ref[...]).astype(BF16)

    def proj(c):
        return _dot(xn, win_ref[:, c * IN_CHUNK:(c + 1) * IN_CHUNK])

    q = proj(0) * SB_QSCALE
    k = proj(1)
    v = proj(2)
    for h in range(SB_HEADS):
        sl = slice(h * SB_HEAD_DIM, (h + 1) * SB_HEAD_DIM)
        q_ref[0, h] = q[:, sl].astype(BF16)
        k_ref[0, h] = k[:, sl].astype(BF16)
        v_ref[0, h] = v[:, sl].astype(BF16)
    kt_ref[0] = k.T
    vt_ref[0] = v.T

    cb = proj(3)
    c = proj(4) * proj(5)

    @pl.when(i == 0)
    def _():
        hist_ref[0:8, :] = jnp.zeros((8, CONV_WIDTH), F32)

    hist_ref[8:8 + tm, :] = c
    c_m1 = hist_ref[pl.ds(7, tm), :]
    c_m2 = hist_ref[pl.ds(6, tm), :]
    w = wconv_ref[...]
    yconv_ref[0] = (cb * (c_m2 * w[0:1] + c_m1 * w[1:2] + c * w[2:3])).astype(BF16)
    tail = c[tm - (CONV_K - 1):tm, :]
    hist_ref[6:8, :] = tail
    cstate_ref[0] = tail

    xq = proj(6) * XA_SCALE
    for h in range(XA_HEADS):
        sl = slice(h * XA_HEAD_DIM, (h + 1) * XA_HEAD_DIM)
        s = _dot_nt(xq[:, sl].astype(BF16), mk_ref[0, :, sl])
        e = jnp.exp(s - jnp.max(s, axis=-1, keepdims=True))
        o = _dot(e.astype(BF16), mv_ref[0, :, sl]) / jnp.sum(e, axis=-1, keepdims=True)
        yxa_ref[0, :, sl] = o.astype(BF16)


def _proj_prompt(x, g, w_in_b, w_conv, mk_b, mv_b):
    nb, seq, _ = x.shape
    tm = PROMPT_TILE
    head_spec = pl.BlockSpec((1, SB_HEADS, tm, SB_HEAD_DIM), lambda b, i: (b, 0, i, 0))
    t_spec = pl.BlockSpec((1, SB_WIDTH, tm), lambda b, i: (b, 0, i))
    row_spec = lambda w: pl.BlockSpec((1, tm, w), lambda b, i: (b, i, 0))
    mem_spec = pl.BlockSpec((1, N_MEM, XA_WIDTH), lambda b, i: (b, 0, 0))
    head_shape = jax.ShapeDtypeStruct((nb, SB_HEADS, seq, SB_HEAD_DIM), BF16)
    t_shape = jax.ShapeDtypeStruct((nb, SB_WIDTH, seq), F32)
    return pl.pallas_call(
        _proj_prompt_kernel,
        grid=(nb, seq // tm),
        in_specs=[row_spec(D_MODEL), _resident(g.shape), _resident(w_in_b.shape),
                  _resident(w_conv.shape), mem_spec, mem_spec],
        out_specs=(head_spec, head_spec, head_spec, t_spec, t_spec, row_spec(CONV_WIDTH),
                   pl.BlockSpec((1, CONV_K - 1, CONV_WIDTH), lambda b, i: (b, 0, 0)),
                   row_spec(XA_WIDTH)),
        out_shape=(head_shape, head_shape, head_shape, t_shape, t_shape,
                   jax.ShapeDtypeStruct((nb, seq, CONV_WIDTH), BF16),
                   jax.ShapeDtypeStruct((nb, CONV_K - 1, CONV_WIDTH), F32),
                   jax.ShapeDtypeStruct((nb, seq, XA_WIDTH), BF16)),
        scratch_shapes=[pltpu.VMEM((tm + 8, CONV_WIDTH), F32)],
        compiler_params=_params("arbitrary", "arbitrary"),
        name="proj_prompt",
    )(x, g, w_in_b, w_conv, mk_b, mv_b)


def _sb_prompt_kernel(bias_ref, q_ref, k_ref, v_ref, negu_ref, o_ref, p_scr, zmp_scr, w_scr):
    ng, t = q_ref.shape[1], q_ref.shape[2]
    hg = pl.program_id(1)
    i = pl.program_id(2)
    negu = negu_ref[...]

    def rows(j):
        return pl.ds(pl.multiple_of(j * t, t), t)

    def scores(g, j):
        z = _dot_nt(q_ref[0, g], k_ref[0, g, rows(j), :]) + bias_ref[hg * ng + g] * LOG2E
        p = _softplus2(z)
        return p, z - p

    row = lax.broadcasted_iota(jnp.int32, (t, t), 0)
    col = lax.broadcasted_iota(jnp.int32, (t, t), 1)
    causal = col < row
    for g in range(ng):
        p, zmp = scores(g, i)
        p_scr[g] = jnp.where(causal, p, 0.0).astype(BF16)
        zmp_scr[g] = jnp.where(causal, zmp, MASKED_LOG2)
        w_scr[g] = jnp.zeros((t, t), BF16)

    def trip(n, state):
        j_prev = jnp.minimum(i - n + 1, i)
        j_next = jnp.maximum(i - n - 1, 0)
        new = []
        for g in range(ng):
            acc, carry = state[g]
            acc = acc + _dot(w_scr[g], v_ref[0, g, rows(j_prev), :])
            p_cur = p_scr[g]
            after = _dot(p_cur, negu)
            w = jnp.exp2(zmp_scr[g] + (after + carry))
            carry = carry + (after[:, 0:1] - p_cur[:, 0:1].astype(F32))
            p, zmp = scores(g, j_next)
            w_scr[g] = w.astype(BF16)
            p_scr[g] = p.astype(BF16)
            zmp_scr[g] = zmp
            new.append((acc, carry))
        return tuple(new)

    zero = (jnp.zeros((t, SB_HEAD_DIM), F32), jnp.zeros((t, 1), F32))
    state = lax.fori_loop(0, i + 1, trip, (zero,) * ng)
    for g in range(ng):
        acc = state[g][0] + _dot(w_scr[g], v_ref[0, g, 0:t, :])
        o_ref[0, g] = acc.astype(BF16)


def _sb_prompt(b_sb, q, k, v):
    nb, nh, seq, dh = q.shape
    t, ng = SB_TILE, SB_HEAD_GROUP
    kv_spec = pl.BlockSpec((1, ng, seq, dh), lambda b, h, i: (b, h, 0, 0))
    qo_spec = pl.BlockSpec((1, ng, t, dh), lambda b, h, i: (b, h, i, 0))
    return pl.pallas_call(
        _sb_prompt_kernel,
        grid=(nb, nh // ng, seq // t),
        in_specs=[pl.BlockSpec(memory_space=pltpu.SMEM), qo_spec, kv_spec, kv_spec,
                  _resident((t, t))],
        out_specs=qo_spec,
        out_shape=jax.ShapeDtypeStruct((nb, nh, seq, dh), BF16),
        scratch_shapes=[pltpu.VMEM((ng, t, t), BF16), pltpu.VMEM((ng, t, t), F32),
                        pltpu.VMEM((ng, t, t), BF16)],
        compiler_params=_params("arbitrary", "arbitrary", "arbitrary"),
        name="sb_prompt",
    )(b_sb, q, k, v, _neg_upper(t))


def _merge_kernel(x_ref, ysb_ref, yconv_ref, yxa_ref, gpre_ref, wgate_ref, bgate_ref,
                  wsb_ref, wcv_ref, wxa_ref, wo_ref, gpost_ref, h_ref):
    x = x_ref[0]
    xn = _rms(x, gpre_ref[...]).astype(BF16)

    def gate(br):
        sl = slice(br * D_MODEL, (br + 1) * D_MODEL)
        return 1.0 / (1.0 + jnp.exp(-(_dot(xn, wgate_ref[:, sl]) + bgate_ref[:, sl])))

    y_sb = _dot(ysb_ref[0, 0], wsb_ref[0:SB_HEAD_DIM, :])
    for h in range(1, SB_HEADS):
        y_sb = y_sb + _dot(ysb_ref[0, h], wsb_ref[h * SB_HEAD_DIM:(h + 1) * SB_HEAD_DIM, :])
    m = gate(0) * y_sb
    m = m + gate(1) * _dot(yconv_ref[0], wcv_ref[...])
    m = m + gate(2) * _dot(yxa_ref[0], wxa_ref[...])
    mo = _dot(m.astype(BF16), wo_ref[...])
    h_ref[0] = x + _rms(mo, gpost_ref[...])


def _merge(x, ysb, yconv, yxa, g_pre, w_gate_b, b_gate, w_sb_b, w_cv_b, w_xa_b, w_o_b, g_post, tm):
    nb, seq, _ = x.shape
    row_spec = lambda w: pl.BlockSpec((1, tm, w), lambda b, i: (b, i, 0))
    consts = (g_pre, w_gate_b, b_gate, w_sb_b, w_cv_b, w_xa_b, w_o_b, g_post)
    return pl.pallas_call(
        _merge_kernel,
        grid=(nb, seq // tm),
        in_specs=[row_spec(D_MODEL),
                  pl.BlockSpec((1, SB_HEADS, tm, SB_HEAD_DIM), lambda b, i: (b, 0, i, 0)),
                  row_spec(CONV_WIDTH), row_spec(XA_WIDTH)] + [_resident(c.shape) for c in consts],
        out_specs=row_spec(D_MODEL),
        out_shape=jax.ShapeDtypeStruct((nb, seq, D_MODEL), F32),
        compiler_params=_params("arbitrary", "arbitrary"),
        name="merge",
    )(x, ysb, yconv, yxa, *consts)


def _ffn_kernel(h_ref, gpre_ref, wup_ref, wdown_ref, gpost_ref, o_ref):
    h = h_ref[0]
    hn = _rms(h, gpre_ref[...]).astype(BF16)
    f = jnp.zeros(h.shape, F32)
    for c in range(D_FF // FF_CHUNK):
        sl = slice(c * FF_CHUNK, (c + 1) * FF_CHUNK)
        a = jnp.maximum(_dot(hn, wup_ref[:, sl]), 0.0)
        f = f + _dot((a * a).astype(BF16), wdown_ref[sl, :])
    o_ref[0] = h + _rms(f, gpost_ref[...])


def _ffn(h, g_pre, w_up_b, w_down_b, g_post, tm):
    nb, seq, _ = h.shape
    row_spec = pl.BlockSpec((1, tm, D_MODEL), lambda b, i: (b, i, 0))
    consts = (g_pre, w_up_b, w_down_b, g_post)
    return pl.pallas_call(
        _ffn_kernel,
        grid=(nb, seq // tm),
        in_specs=[row_spec] + [_resident(c.shape) for c in consts],
        out_specs=row_spec,
        out_shape=jax.ShapeDtypeStruct((nb, seq, D_MODEL), F32),
        compiler_params=_params("arbitrary", "arbitrary"),
        name="ffn",
    )(h, *consts)


def _proj_sample_kernel(x_ref, g_ref, win_ref, wconv_ref, s0_ref, s1_ref,
                        q_ref, k_ref, v_ref, yconv_ref, c_ref, xq_ref):
    xn = _rms(x_ref[...], g_ref[...]).astype(BF16)

    def proj(c):
        return _dot(xn, win_ref[:, c * IN_CHUNK:(c + 1) * IN_CHUNK])

    q_ref[...] = proj(0) * SB_QSCALE
    k_ref[...] = proj(1)
    v_ref[...] = proj(2)
    cb = proj(3)
    c = proj(4) * proj(5)
    w = wconv_ref[...]
    yconv_ref[...] = (cb * (s0_ref[...] * w[0:1] + s1_ref[...] * w[1:2] + c * w[2:3])).astype(BF16)
    c_ref[...] = c
    xq_ref[...] = proj(6) * XA_SCALE


def _proj_sample(x2d, g, w_in_b, w_conv, s0, s1):
    n = x2d.shape[0]
    wide = jax.ShapeDtypeStruct((n, IN_CHUNK), F32)
    return pl.pallas_call(
        _proj_sample_kernel,
        out_shape=(wide, wide, wide, jax.ShapeDtypeStruct((n, CONV_WIDTH), BF16), wide, wide),
        compiler_params=pltpu.CompilerParams(vmem_limit_bytes=VMEM_LIMIT),
        name="proj_sample",
    )(x2d, g, w_in_b, w_conv, s0, s1)


def _xattn_sample_kernel(xq_ref, mk_ref, mv_ref, o_ref):
    xq = xq_ref[0]
    for h in range(XA_HEADS):
        sl = slice(h * XA_HEAD_DIM, (h + 1) * XA_HEAD_DIM)
        qh = jnp.broadcast_to(xq[:, sl], (8, XA_HEAD_DIM)).astype(BF16)
        s = _dot_nt(qh, mk_ref[0, :, sl].astype(BF16))
        e = jnp.exp(s - jnp.max(s, axis=-1, keepdims=True))
        o = _dot(e.astype(BF16), mv_ref[0, :, sl].astype(BF16)) / jnp.sum(e, axis=-1, keepdims=True)
        o_ref[0, :, sl] = o[0:1].astype(BF16)


def _xattn_sample(xq, mk, mv):
    n = xq.shape[0]
    q_spec = pl.BlockSpec((1, 1, XA_WIDTH), lambda b: (b, 0, 0))
    m_spec = pl.BlockSpec((1, N_MEM, XA_WIDTH), lambda b: (b, 0, 0))
    return pl.pallas_call(
        _xattn_sample_kernel,
        grid=(n,),
        in_specs=[q_spec, m_spec, m_spec],
        out_specs=q_spec,
        out_shape=jax.ShapeDtypeStruct((n, 1, XA_WIDTH), BF16),
        compiler_params=_params("arbitrary"),
        name="xattn_sample",
    )(xq.reshape(n, 1, XA_WIDTH), mk, mv)


def _sb_decode_kernel(pt_ref, q_ref, bias_ref, negu_ref, *refs):
    del pt_ref
    n = DECODE_PAGES
    k_refs, v_refs = refs[:n], refs[n:2 * n]
    o_ref, acc_ref, carry_ref = refs[2 * n:]
    g = pl.program_id(1)

    @pl.when(g == 0)
    def _():
        acc_ref[...] = jnp.zeros(acc_ref.shape, F32)
        carry_ref[...] = jnp.zeros(carry_ref.shape, F32)

    qb = q_ref[0]
    z = jnp.concatenate([jnp.sum(qb * k_refs[p][0], axis=1) for p in range(n)], axis=0)
    z = z + bias_ref[...]
    p_all = _softplus2(z)
    negu = negu_ref[...]
    p_hi = p_all.astype(BF16)
    rest = p_all - p_hi.astype(F32)
    p_mid = rest.astype(BF16)
    p_lo = (rest - p_mid.astype(F32)).astype(BF16)
    after = _dot(p_hi, negu) + _dot(p_mid, negu) + _dot(p_lo, negu)
    total = after[:, 0:1] - p_all[:, 0:1]
    carry = carry_ref[...]
    ws = []
    for p in range(n):
        sl = slice(p * SB_HEADS, (p + 1) * SB_HEADS)
        ws.append(jnp.exp2((z[sl] - p_all[sl]) + (after[sl] + carry)))
        carry = carry + total[sl]
    carry_ref[...] = carry
    for h in range(SB_HEADS):
        acc = acc_ref[h]
        for p in range(n):
            acc = acc + ws[p][h:h + 1, :] * v_refs[p][0, h]
        acc_ref[h] = acc

    @pl.when(g == pl.num_programs(1) - 1)
    def _():
        o_ref[0] = jnp.sum(acc_ref[...], axis=2, keepdims=True)


def _sb_decode(page_table, q, b_sb, kt_pages, vt_pages):
    nseq, n_pages = page_table.shape
    n = DECODE_PAGES
    bias = jnp.tile(b_sb.reshape(SB_HEADS, 1) * LOG2E, (n, 1))

    def page_spec(p):
        return pl.BlockSpec(
            (1, SB_HEADS, SB_HEAD_DIM, PAGE_SIZE),
            lambda b, g, pt: (pt[b, n_pages - 1 - (g * n + p)], 0, 0, 0))

    q_spec = pl.BlockSpec((1, SB_HEADS, SB_HEAD_DIM, 1), lambda b, g, pt: (b, 0, 0, 0))
    grid_spec = pltpu.PrefetchScalarGridSpec(
        num_scalar_prefetch=1,
        grid=(nseq, n_pages // n),
        in_specs=[q_spec,
                  pl.BlockSpec((n * SB_HEADS, 1), lambda b, g, pt: (0, 0)),
                  pl.BlockSpec((PAGE_SIZE, PAGE_SIZE), lambda b, g, pt: (0, 0))]
                 + [page_spec(p) for p in range(n)] * 2,
        out_specs=q_spec,
        scratch_shapes=[pltpu.VMEM((SB_HEADS, SB_HEAD_DIM, PAGE_SIZE), F32),
                        pltpu.VMEM((SB_HEADS, 1), F32)],
    )
    return pl.pallas_call(
        _sb_decode_kernel,
        grid_spec=grid_spec,
        out_shape=jax.ShapeDtypeStruct((nseq, SB_HEADS, SB_HEAD_DIM, 1), F32),
        compiler_params=_params("arbitrary", "arbitrary"),
        name="sb_decode",
    )(page_table, q.reshape(nseq, SB_HEADS, SB_HEAD_DIM, 1), bias, _neg_upper(PAGE_SIZE),
      *([kt_pages] * n), *([vt_pages] * n))


def kernel(x_prompt, x_sample, mem_prompt, cache_k_pages, cache_v_pages, page_table, cache_mem_k,
           cache_mem_v, state_conv, g_mix_pre, w_in, b_sb, w_conv, g_mem, w_mem_kv, w_gate, b_gate,
           w_sb_o, w_conv_o, w_xa_o, w_o, g_mix_post, g_ffn_pre, w_up, w_down, g_ffn_post):
    depth = w_in.shape[0]
    assert depth == 1, "single-layer step"
    nb, seq, _ = x_prompt.shape
    nseq = x_sample.shape[0]
    bf = lambda a: a[0].astype(BF16)
    w_in_b, w_gate_b, w_mem_b = bf(w_in), bf(w_gate), bf(w_mem_kv)
    w_sb_b, w_cv_b, w_xa_b, w_o_b = bf(w_sb_o), bf(w_conv_o), bf(w_xa_o), bf(w_o)
    w_up_b, w_down_b = bf(w_up), bf(w_down)
    tail = (g_mix_pre, w_gate_b, b_gate, w_sb_b, w_cv_b, w_xa_b, w_o_b, g_mix_post)
    ffn_w = (g_ffn_pre, w_up_b, w_down_b, g_ffn_post)

    kv, kv_b = _memkv(mem_prompt.reshape(nb * N_MEM, D_MODEL), g_mem, w_mem_b)
    mk = kv[:, :XA_WIDTH].reshape(1, nb, N_MEM, XA_HEADS, XA_HEAD_DIM)
    mv = kv[:, XA_WIDTH:].reshape(1, nb, N_MEM, XA_HEADS, XA_HEAD_DIM)
    mk_b = kv_b[:, :XA_WIDTH].reshape(nb, N_MEM, XA_WIDTH)
    mv_b = kv_b[:, XA_WIDTH:].reshape(nb, N_MEM, XA_WIDTH)
    q_p, k_p, v_p, kt, vt, yconv_p, cstate_p, yxa_p = _proj_prompt(
        x_prompt, g_mix_pre, w_in_b, w_conv[0], mk_b, mv_b)
    ysb_p = _sb_prompt(b_sb[0], q_p, k_p, v_p)
    h_p = _merge(x_prompt, ysb_p, yconv_p, yxa_p, *tail, tm=PROMPT_TILE)
    y_p = _ffn(h_p, *ffn_w, tm=PROMPT_TILE)
    to_cache = lambda a: jnp.transpose(
        a.reshape(nb, SB_HEADS, SB_HEAD_DIM, seq), (0, 3, 1, 2))[None]
    k_prompt, v_prompt = to_cache(kt), to_cache(vt)

    xs = x_sample.reshape(nseq, D_MODEL)
    q_s, k_s, v_s, yconv_s, c_s, xq_s = _proj_sample(
        xs, g_mix_pre, w_in_b, w_conv[0], state_conv[0, :, 0, :], state_conv[0, :, 1, :])
    kt_pages = jnp.transpose(cache_k_pages[0], (0, 2, 3, 1))
    vt_pages = jnp.transpose(cache_v_pages[0], (0, 2, 3, 1))
    ysb_s = _sb_decode(page_table, q_s, b_sb[0], kt_pages, vt_pages)
    ysb_s = jnp.transpose(ysb_s.reshape(nseq, SB_HEADS, SB_HEAD_DIM), (1, 0, 2))[None].astype(BF16)
    yxa_s = _xattn_sample(xq_s, cache_mem_k[0].reshape(nseq, N_MEM, XA_WIDTH),
                          cache_mem_v[0].reshape(nseq, N_MEM, XA_WIDTH))
    h_s = _merge(xs[None], ysb_s, yconv_s[None], yxa_s.reshape(1, nseq, XA_WIDTH), *tail, tm=nseq)
    y_s = _ffn(h_s, *ffn_w, tm=nseq)
    conv_sample = jnp.stack([state_conv[0, :, 1, :], c_s], axis=1)[None]
    to_tok = lambda a: a.reshape(1, nseq, 1, SB_HEADS, SB_HEAD_DIM)

    return (y_p, y_s.reshape(nseq, 1, D_MODEL), k_prompt, v_prompt, cstate_p[None], mk, mv,
            to_tok(k_s), to_tok(v_s), conv_sample)
```

```python
import functools
import math

import jax
import jax.numpy as jnp
from jax import lax
from jax.experimental import pallas as pl
from jax.experimental.pallas import tpu as pltpu

F32 = jnp.float32
BF16 = jnp.bfloat16

D_MODEL = 1024
SB_HEADS = 8
SB_HEAD_DIM = 64
SB_WIDTH = SB_HEADS * SB_HEAD_DIM
CONV_WIDTH = 512
CONV_K = 3
XA_HEADS = 4
XA_HEAD_DIM = 128
XA_WIDTH = XA_HEADS * XA_HEAD_DIM
N_MEM = 256
D_FF = 4 * D_MODEL
N_BRANCH = 3
RMS_EPS = 1e-6
PAGE_SIZE = 128
IN_CHUNK = 512
LOG2E = math.log2(math.e)
SB_QSCALE = LOG2E / math.sqrt(SB_HEAD_DIM)
BIAS_TERMS = 3
MASKED_LOG2 =-1e30
XA_SCALE = 1.0 / math.sqrt(XA_HEAD_DIM)

V7X_VMEM_BYTES = 64 * 1024 * 1024
VMEM_LIMIT = V7X_VMEM_BYTES - 8 * 1024 * 1024

PROMPT_TILE = 512
SB_TILE = 256
SB_HEAD_GROUP = 8
DECODE_PAGES = 16
FF_CHUNK = 512


def _params(*sem):
    return pltpu.CompilerParams(dimension_semantics=sem, vmem_limit_bytes=VMEM_LIMIT)


def _resident(shape):
    nd = len(shape)
    return pl.BlockSpec(shape, lambda *_: (0,) * nd, pipeline_mode=pl.Buffered(1))


def _rms(x, g):
    return x * lax.rsqrt(jnp.mean(x * x, axis=-1, keepdims=True) + RMS_EPS) * g


def _dot(a, b):
    return jnp.dot(a, b, preferred_element_type=F32)


def _dot_nt(a, b):
    return lax.dot_general(a, b, (((1,), (1,)), ((), ())), preferred_element_type=F32)


def _softplus2(z2):
    return jnp.maximum(z2, jnp.log2(1.0 + jnp.exp2(jnp.minimum(z2, 64.0))))


def _bias_columns(b_sb):
    rest = b_sb.astype(F32) * LOG2E
    cols = []
    for _ in range(BIAS_TERMS):
        piece = rest.astype(BF16).astype(F32)
        cols.append(piece)
        rest = rest - piece
    pad = jnp.zeros((b_sb.shape[0], SB_HEAD_DIM - BIAS_TERMS), F32)
    return jnp.concatenate([jnp.stack(cols, axis=1), pad], axis=1)


def _neg_upper(n):
    s = lax.broadcasted_iota(jnp.int32, (n, n), 0)
    j = lax.broadcasted_iota(jnp.int32, (n, n), 1)
    return jnp.where(s > j, -1.0, 0.0).astype(BF16)


def _memkv_kernel(mem_ref, g_ref, w_ref, kv_ref, kvb_ref):
    mn = _rms(mem_ref[...], g_ref[...]).astype(BF16)
    kv = _dot(mn, w_ref[...])
    kv_ref[...] = kv
    kvb_ref[...] = kv.astype(BF16)


def _memkv(mem2d, g_mem, w_mem_kv_b):
    rows = mem2d.shape[0]
    return pl.pallas_call(
        _memkv_kernel,
        out_shape=(jax.ShapeDtypeStruct((rows, 2 * XA_WIDTH), F32),
                   jax.ShapeDtypeStruct((rows, 2 * XA_WIDTH), BF16)),
        compiler_params=pltpu.CompilerParams(vmem_limit_bytes=VMEM_LIMIT),
        name="memkv",
    )(mem2d, g_mem, w_mem_kv_b)


def _proj_prompt_kernel(x_ref, g_ref, win_ref, wconv_ref, mk_ref, mv_ref, kpad_ref,
                        q_ref, k_ref, v_ref, kt_ref, vt_ref, yconv_ref, cstate_ref, yxa_ref,
                        hist_ref):
    tm = x_ref.shape[1]
    i = pl.program_id(1)
    xn = _rms(x_ref[0], g_ref[...]).astype(BF16)

    def proj(c):
        return _dot(xn, win_ref[:, c * IN_CHUNK:(c + 1) * IN_CHUNK])

    q = proj(0) * SB_QSCALE
    k = proj(1)
    v = proj(2)
    lane = lax.broadcasted_iota(jnp.int32, (tm, SB_HEAD_DIM), 1)
    qpad = jnp.where(lane < BIAS_TERMS, 1.0, 0.0)
    for h in range(SB_HEADS):
        sl = slice(h * SB_HEAD_DIM, (h + 1) * SB_HEAD_DIM)
        kpad = jnp.broadcast_to(kpad_ref[h:h + 1, :], (tm, SB_HEAD_DIM))
        q_ref[0, h] = jnp.concatenate([q[:, sl], qpad], axis=1).astype(BF16)
        k_ref[0, h] = jnp.concatenate([k[:, sl], kpad], axis=1).astype(BF16)
        v_ref[0, h] = v[:, sl].astype(BF16)
    kt_ref[0] = k.T
    vt_ref[0] = v.T

    cb = proj(3)
    c = proj(4) * proj(5)

    @pl.when(i == 0)
    def _():
        hist_ref[0:8, :] = jnp.zeros((8, CONV_WIDTH), F32)

    hist_ref[8:8 + tm, :] = c
    c_m1 = hist_ref[pl.ds(7, tm), :]
    c_m2 = hist_ref[pl.ds(6, tm), :]
    w = wconv_ref[...]
    yconv_ref[0] = (cb * (c_m2 * w[0:1] + c_m1 * w[1:2] + c * w[2:3])).astype(BF16)
    tail = c[tm - (CONV_K - 1):tm, :]
    hist_ref[6:8, :] = tail
    cstate_ref[0] = tail

    xq = proj(6) * XA_SCALE
    for h in range(XA_HEADS):
        sl = slice(h * XA_HEAD_DIM, (h + 1) * XA_HEAD_DIM)
        s = _dot_nt(xq[:, sl].astype(BF16), mk_ref[0, :, sl])
        e = jnp.exp(s - jnp.max(s, axis=-1, keepdims=True))
        o = _dot(e.astype(BF16), mv_ref[0, :, sl]) / jnp.sum(e, axis=-1, keepdims=True)
        yxa_ref[0, :, sl] = o.astype(BF16)


def _proj_prompt(x, g, w_in_b, w_conv, mk_b, mv_b, kpad):
    nb, seq, _ = x.shape
    tm = PROMPT_TILE
    head_spec = pl.BlockSpec((1, SB_HEADS, tm, SB_HEAD_DIM), lambda b, i: (b, 0, i, 0))
    wide_spec = pl.BlockSpec((1, SB_HEADS, tm, 2 * SB_HEAD_DIM), lambda b, i: (b, 0, i, 0))
    t_spec = pl.BlockSpec((1, SB_WIDTH, tm), lambda b, i: (b, 0, i))
    row_spec = lambda w: pl.BlockSpec((1, tm, w), lambda b, i: (b, i, 0))
    mem_spec = pl.BlockSpec((1, N_MEM, XA_WIDTH), lambda b, i: (b, 0, 0))
    head_shape = jax.ShapeDtypeStruct((nb, SB_HEADS, seq, SB_HEAD_DIM), BF16)
    wide_shape = jax.ShapeDtypeStruct((nb, SB_HEADS, seq, 2 * SB_HEAD_DIM), BF16)
    t_shape = jax.ShapeDtypeStruct((nb, SB_WIDTH, seq), F32)
    return pl.pallas_call(
        _proj_prompt_kernel,
        grid=(nb, seq // tm),
        in_specs=[row_spec(D_MODEL), _resident(g.shape), _resident(w_in_b.shape),
                  _resident(w_conv.shape), mem_spec, mem_spec, _resident(kpad.shape)],
        out_specs=(wide_spec, wide_spec, head_spec, t_spec, t_spec, row_spec(CONV_WIDTH),
                   pl.BlockSpec((1, CONV_K - 1, CONV_WIDTH), lambda b, i: (b, 0, 0)),
                   row_spec(XA_WIDTH)),
        out_shape=(wide_shape, wide_shape, head_shape, t_shape, t_shape,
                   jax.ShapeDtypeStruct((nb, seq, CONV_WIDTH), BF16),
                   jax.ShapeDtypeStruct((nb, CONV_K - 1, CONV_WIDTH), F32),
                   jax.ShapeDtypeStruct((nb, seq, XA_WIDTH), BF16)),
        scratch_shapes=[pltpu.VMEM((tm + 8, CONV_WIDTH), F32)],
        compiler_params=_params("arbitrary", "arbitrary"),
        name="proj_prompt",
    )(x, g, w_in_b, w_conv, mk_b, mv_b, kpad)


def _sb_prompt_kernel(q_ref, k_ref, v_ref, negu_ref, o_ref, p_scr, zmp_scr, w_scr):
    ng, t = q_ref.shape[1], q_ref.shape[2]
    i = pl.program_id(2)
    negu = negu_ref[...]

    def rows(j):
        return pl.ds(pl.multiple_of(j * t, t), t)

    def scores(g, j):
        z = _dot_nt(q_ref[0, g], k_ref[0, g, rows(j), :])
        p = _softplus2(z)
        return p, z - p

    row = lax.broadcasted_iota(jnp.int32, (t, t), 0)
    col = lax.broadcasted_iota(jnp.int32, (t, t), 1)
    causal = col < row
    for g in range(ng):
        p, zmp = scores(g, i)
        p_scr[g] = jnp.where(causal, p, 0.0).astype(BF16)
        zmp_scr[g] = jnp.where(causal, zmp, MASKED_LOG2)
        w_scr[g] = jnp.zeros((t, t), BF16)

    def trip(n, state):
        j_prev = jnp.minimum(i - n + 1, i)
        j_next = jnp.maximum(i - n - 1, 0)
        new = []
        for g in range(ng):
            acc, carry = state[g]
            acc = acc + _dot(w_scr[g], v_ref[0, g, rows(j_prev), :])
            p_cur = p_scr[g]
            after = _dot(p_cur, negu)
            keep = after + carry
            w = jnp.exp2(zmp_scr[g] + keep)
            carry = keep[:, 0:1] - p_cur[:, 0:1].astype(F32)
            p, zmp = scores(g, j_next)
            w_scr[g] = w.astype(BF16)
            p_scr[g] = p.astype(BF16)
            zmp_scr[g] = zmp
            new.append((acc, carry))
        return tuple(new)

    zero = (jnp.zeros((t, SB_HEAD_DIM), F32), jnp.zeros((t, 1), F32))
    state = lax.fori_loop(0, i + 1, trip, (zero,) * ng)
    for g in range(ng):
        acc = state[g][0] + _dot(w_scr[g], v_ref[0, g, 0:t, :])
        o_ref[0, g] = acc.astype(BF16)


def _sb_prompt(q, k, v):
    nb, nh, seq, dh = v.shape
    t, ng = SB_TILE, SB_HEAD_GROUP
    kv_spec = lambda w: pl.BlockSpec((1, ng, seq, w), lambda b, h, i: (b, h, 0, 0),
                                     pipeline_mode=pl.Buffered(1))
    qo_spec = lambda w: pl.BlockSpec((1, ng, t, w), lambda b, h, i: (b, h, i, 0))
    return pl.pallas_call(
        _sb_prompt_kernel,
        grid=(nb, nh // ng, seq // t),
        in_specs=[qo_spec(q.shape[3]), kv_spec(k.shape[3]), kv_spec(dh), _resident((t, t))],
        out_specs=qo_spec(dh),
        out_shape=jax.ShapeDtypeStruct((nb, nh, seq, dh), BF16),
        scratch_shapes=[pltpu.VMEM((ng, t, t), BF16), pltpu.VMEM((ng, t, t), F32),
                        pltpu.VMEM((ng, t, t), BF16)],
        compiler_params=_params("arbitrary", "arbitrary", "arbitrary"),
        name="sb_prompt",
    )(q, k, v, _neg_upper(t))


def _merge_kernel(x_ref, ysb_ref, yconv_ref, yxa_ref, gpre_ref, wgate_ref, bgate_ref,
                  wsb_ref, wcv_ref, wxa_ref, wo_ref, gpost_ref, h_ref):
    x = x_ref[0]
    xn = _rms(x, gpre_ref[...]).astype(BF16)

    def gate(br):
        sl = slice(br * D_MODEL, (br + 1) * D_MODEL)
        return 1.0 / (1.0 + jnp.exp(-(_dot(xn, wgate_ref[:, sl]) + bgate_ref[:, sl])))

    y_sb = _dot(ysb_ref[0, 0], wsb_ref[0:SB_HEAD_DIM, :])
    for h in range(1, SB_HEADS):
        y_sb = y_sb + _dot(ysb_ref[0, h], wsb_ref[h * SB_HEAD_DIM:(h + 1) * SB_HEAD_DIM, :])
    m = gate(0) * y_sb
    m = m + gate(1) * _dot(yconv_ref[0], wcv_ref[...])
    m = m + gate(2) * _dot(yxa_ref[0], wxa_ref[...])
    mo = _dot(m.astype(BF16), wo_ref[...])
    h_ref[0] = x + _rms(mo, gpost_ref[...])


def _merge(x, ysb, yconv, yxa, g_pre, w_gate_b, b_gate, w_sb_b, w_cv_b, w_xa_b, w_o_b, g_post, tm):
    nb, seq, _ = x.shape
    row_spec = lambda w: pl.BlockSpec((1, tm, w), lambda b, i: (b, i, 0))
    consts = (g_pre, w_gate_b, b_gate, w_sb_b, w_cv_b, w_xa_b, w_o_b, g_post)
    return pl.pallas_call(
        _merge_kernel,
        grid=(nb, seq // tm),
        in_specs=[row_spec(D_MODEL),
                  pl.BlockSpec((1, SB_HEADS, tm, SB_HEAD_DIM), lambda b, i: (b, 0, i, 0)),
                  row_spec(CONV_WIDTH), row_spec(XA_WIDTH)] + [_resident(c.shape) for c in consts],
        out_specs=row_spec(D_MODEL),
        out_shape=jax.ShapeDtypeStruct((nb, seq, D_MODEL), F32),
        compiler_params=_params("arbitrary", "arbitrary"),
        name="merge",
    )(x, ysb, yconv, yxa, *consts)


def _ffn_kernel(h_ref, gpre_ref, wup_ref, wdown_ref, gpost_ref, o_ref):
    h = h_ref[0]
    hn = _rms(h, gpre_ref[...]).astype(BF16)
    f = jnp.zeros(h.shape, F32)
    for c in range(D_FF // FF_CHUNK):
        sl = slice(c * FF_CHUNK, (c + 1) * FF_CHUNK)
        a = jnp.maximum(_dot(hn, wup_ref[:, sl]), 0.0)
        f = f + _dot((a * a).astype(BF16), wdown_ref[sl, :])
    o_ref[0] = h + _rms(f, gpost_ref[...])


def _ffn(h, g_pre, w_up_b, w_down_b, g_post, tm):
    nb, seq, _ = h.shape
    row_spec = pl.BlockSpec((1, tm, D_MODEL), lambda b, i: (b, i, 0))
    consts = (g_pre, w_up_b, w_down_b, g_post)
    return pl.pallas_call(
        _ffn_kernel,
        grid=(nb, seq // tm),
        in_specs=[row_spec] + [_resident(c.shape) for c in consts],
        out_specs=row_spec,
        out_shape=jax.ShapeDtypeStruct((nb, seq, D_MODEL), F32),
        compiler_params=_params("arbitrary", "arbitrary"),
        name="ffn",
    )(h, *consts)


def _proj_sample_kernel(x_ref, g_ref, win_ref, wconv_ref, s0_ref, s1_ref,
                        q_ref, k_ref, v_ref, yconv_ref, c_ref, xq_ref):
    xn = _rms(x_ref[...], g_ref[...]).astype(BF16)

    def proj(c):
        return _dot(xn, win_ref[:, c * IN_CHUNK:(c + 1) * IN_CHUNK])

    q_ref[...] = proj(0) * SB_QSCALE
    k_ref[...] = proj(1)
    v_ref[...] = proj(2)
    cb = proj(3)
    c = proj(4) * proj(5)
    w = wconv_ref[...]
    yconv_ref[...] = (cb * (s0_ref[...] * w[0:1] + s1_ref[...] * w[1:2] + c * w[2:3])).astype(BF16)
    c_ref[...] = c
    xq_ref[...] = proj(6) * XA_SCALE


def _proj_sample(x2d, g, w_in_b, w_conv, s0, s1):
    n = x2d.shape[0]
    wide = jax.ShapeDtypeStruct((n, IN_CHUNK), F32)
    return pl.pallas_call(
        _proj_sample_kernel,
        out_shape=(wide, wide, wide, jax.ShapeDtypeStruct((n, CONV_WIDTH), BF16), wide, wide),
        compiler_params=pltpu.CompilerParams(vmem_limit_bytes=VMEM_LIMIT),
        name="proj_sample",
    )(x2d, g, w_in_b, w_conv, s0, s1)


def _xattn_sample_kernel(xq_ref, mk_ref, mv_ref, o_ref):
    xq = xq_ref[0]
    for h in range(XA_HEADS):
        sl = slice(h * XA_HEAD_DIM, (h + 1) * XA_HEAD_DIM)
        qh = jnp.broadcast_to(xq[:, sl], (8, XA_HEAD_DIM)).astype(BF16)
        s = _dot_nt(qh, mk_ref[0, :, sl].astype(BF16))
        e = jnp.exp(s - jnp.max(s, axis=-1, keepdims=True))
        o = _dot(e.astype(BF16), mv_ref[0, :, sl].astype(BF16)) / jnp.sum(e, axis=-1, keepdims=True)
        o_ref[0, :, sl] = o[0:1].astype(BF16)


def _xattn_sample(xq, mk, mv):
    n = xq.shape[0]
    q_spec = pl.BlockSpec((1, 1, XA_WIDTH), lambda b: (b, 0, 0))
    m_spec = pl.BlockSpec((1, N_MEM, XA_WIDTH), lambda b: (b, 0, 0))
    return pl.pallas_call(
        _xattn_sample_kernel,
        grid=(n,),
        in_specs=[q_spec, m_spec, m_spec],
        out_specs=q_spec,
        out_shape=jax.ShapeDtypeStruct((n, 1, XA_WIDTH), BF16),
        compiler_params=_params("arbitrary"),
        name="xattn_sample",
    )(xq.reshape(n, 1, XA_WIDTH), mk, mv)


def _sb_decode_kernel(pt_ref, q_ref, bias_ref, negu_ref, *refs):
    del pt_ref
    n = DECODE_PAGES
    k_refs, v_refs = refs[:n], refs[n:2 * n]
    o_ref, acc_ref, carry_ref, w_ref = refs[2 * n:]
    g = pl.program_id(1)

    @pl.when(g == 0)
    def _():
        acc_ref[...] = jnp.zeros(acc_ref.shape, F32)
        carry_ref[...] = jnp.zeros(carry_ref.shape, F32)
        w_ref[...] = jnp.zeros(w_ref.shape, F32)

    for h in range(SB_HEADS):
        acc = acc_ref[h]
        for p in range(n):
            acc = acc + w_ref[pl.ds(p * SB_HEADS + h, 1), :] * v_refs[p][0, h]
        acc_ref[h] = acc

    qb = q_ref[0]
    z = jnp.concatenate([jnp.sum(qb * k_refs[p][0], axis=1) for p in range(n)], axis=0)
    z = z + bias_ref[...]
    p_all = _softplus2(z)
    negu = negu_ref[...]
    p_hi = p_all.astype(BF16)
    rest = p_all - p_hi.astype(F32)
    p_mid = rest.astype(BF16)
    p_lo = (rest - p_mid.astype(F32)).astype(BF16)
    after = _dot(p_hi, negu) + _dot(p_mid, negu) + _dot(p_lo, negu)
    total = after[:, 0:1] - p_all[:, 0:1]
    carry = carry_ref[...]
    for p in range(n):
        sl = slice(p * SB_HEADS, (p + 1) * SB_HEADS)
        w_ref[sl, :] = jnp.exp2((z[sl] - p_all[sl]) + (after[sl] + carry))
        carry = carry + total[sl]
    carry_ref[...] = carry

    @pl.when(g == pl.num_programs(1) - 1)
    def _():
        o_ref[0] = jnp.sum(acc_ref[...], axis=2, keepdims=True)


def _sb_decode(page_table, q, b_sb, kt_pages, vt_pages):
    nseq, n_pages = page_table.shape
    n = DECODE_PAGES
    bias = jnp.tile(b_sb.reshape(SB_HEADS, 1) * LOG2E, (n, 1))

    n_groups = n_pages // n

    def page_spec(p, lag):
        def index(b, g, pt):
            group = jnp.clip(g - lag, 0, n_groups - 1)
            return (pt[b, n_pages - 1 - (group * n + p)], 0, 0, 0)
        return pl.BlockSpec((1, SB_HEADS, SB_HEAD_DIM, PAGE_SIZE), index)

    q_spec = pl.BlockSpec((1, SB_HEADS, SB_HEAD_DIM, 1), lambda b, g, pt: (b, 0, 0, 0))
    grid_spec = pltpu.PrefetchScalarGridSpec(
        num_scalar_prefetch=1,
        grid=(nseq, n_groups + 1),
        in_specs=[q_spec,
                  pl.BlockSpec((n * SB_HEADS, 1), lambda b, g, pt: (0, 0)),
                  pl.BlockSpec((PAGE_SIZE, PAGE_SIZE), lambda b, g, pt: (0, 0))]
                 + [page_spec(p, 0) for p in range(n)] + [page_spec(p, 1) for p in range(n)],
        out_specs=q_spec,
        scratch_shapes=[pltpu.VMEM((SB_HEADS, SB_HEAD_DIM, PAGE_SIZE), F32),
                        pltpu.VMEM((SB_HEADS, 1), F32),
                        pltpu.VMEM((n * SB_HEADS, PAGE_SIZE), F32)],
    )
    return pl.pallas_call(
        _sb_decode_kernel,
        grid_spec=grid_spec,
        out_shape=jax.ShapeDtypeStruct((nseq, SB_HEADS, SB_HEAD_DIM, 1), F32),
        compiler_params=_params("arbitrary", "arbitrary"),
        name="sb_decode",
    )(page_table, q.reshape(nseq, SB_HEADS, SB_HEAD_DIM, 1), bias, _neg_upper(PAGE_SIZE),
      *([kt_pages] * n), *([vt_pages] * n))


def kernel(x_prompt, x_sample, mem_prompt, cache_k_pages, cache_v_pages, page_table, cache_mem_k,
           cache_mem_v, state_conv, g_mix_pre, w_in, b_sb, w_conv, g_mem, w_mem_kv, w_gate, b_gate,
           w_sb_o, w_conv_o, w_xa_o, w_o, g_mix_post, g_ffn_pre, w_up, w_down, g_ffn_post):
    depth = w_in.shape[0]
    assert depth == 1, "single-layer step"
    nb, seq, _ = x_prompt.shape
    nseq = x_sample.shape[0]
    bf = lambda a: a[0].astype(BF16)
    w_in_b, w_gate_b, w_mem_b = bf(w_in), bf(w_gate), bf(w_mem_kv)
    w_sb_b, w_cv_b, w_xa_b, w_o_b = bf(w_sb_o), bf(w_conv_o), bf(w_xa_o), bf(w_o)
    w_up_b, w_down_b = bf(w_up), bf(w_down)
    tail = (g_mix_pre, w_gate_b, b_gate, w_sb_b, w_cv_b, w_xa_b, w_o_b, g_mix_post)
    ffn_w = (g_ffn_pre, w_up_b, w_down_b, g_ffn_post)

    kv, kv_b = _memkv(mem_prompt.reshape(nb * N_MEM, D_MODEL), g_mem, w_mem_b)
    mk = kv[:, :XA_WIDTH].reshape(1, nb, N_MEM, XA_HEADS, XA_HEAD_DIM)
    mv = kv[:, XA_WIDTH:].reshape(1, nb, N_MEM, XA_HEADS, XA_HEAD_DIM)
    mk_b = kv_b[:, :XA_WIDTH].reshape(nb, N_MEM, XA_WIDTH)
    mv_b = kv_b[:, XA_WIDTH:].reshape(nb, N_MEM, XA_WIDTH)
    q_p, k_p, v_p, kt, vt, yconv_p, cstate_p, yxa_p = _proj_prompt(
        x_prompt, g_mix_pre, w_in_b, w_conv[0], mk_b, mv_b, _bias_columns(b_sb[0]))
    ysb_p = _sb_prompt(q_p, k_p, v_p)
    h_p = _merge(x_prompt, ysb_p, yconv_p, yxa_p, *tail, tm=PROMPT_TILE)
    y_p = _ffn(h_p, *ffn_w, tm=PROMPT_TILE)
    to_cache = lambda a: jnp.transpose(
        a.reshape(nb, SB_HEADS, SB_HEAD_DIM, seq), (0, 3, 1, 2))[None]
    k_prompt, v_prompt = to_cache(kt), to_cache(vt)

    xs = x_sample.reshape(nseq, D_MODEL)
    q_s, k_s, v_s, yconv_s, c_s, xq_s = _proj_sample(
        xs, g_mix_pre, w_in_b, w_conv[0], state_conv[0, :, 0, :], state_conv[0, :, 1, :])
    kt_pages = jnp.transpose(cache_k_pages[0], (0, 2, 3, 1))
    vt_pages = jnp.transpose(cache_v_pages[0], (0, 2, 3, 1))
    ysb_s = _sb_decode(page_table, q_s, b_sb[0], kt_pages, vt_pages)
    ysb_s = jnp.transpose(ysb_s.reshape(nseq, SB_HEADS, SB_HEAD_DIM), (1, 0, 2))[None].astype(BF16)
    yxa_s = _xattn_sample(xq_s, cache_mem_k[0].reshape(nseq, N_MEM, XA_WIDTH),
                          cache_mem_v[0].reshape(nseq, N_MEM, XA_WIDTH))
    h_s = _merge(xs[None], ysb_s, yconv_s[None], yxa_s.reshape(1, nseq, XA_WIDTH), *tail, tm=nseq)
    y_s = _ffn(h_s, *ffn_w, tm=nseq)
    conv_sample = jnp.stack([state_conv[0, :, 1, :], c_s], axis=1)[None]
    to_tok = lambda a: a.reshape(1, nseq, 1, SB_HEADS, SB_HEAD_DIM)

    return (y_p, y_s.reshape(nseq, 1, D_MODEL), k_prompt, v_prompt, cstate_p[None], mk, mv,
            to_tok(k_s), to_tok(v_s), conv_sample)
```

```python
import functools
import math

import jax
import jax.numpy as jnp
from jax import lax
from jax.experimental import pallas as pl
from jax.experimental.pallas import tpu as pltpu

F32 = jnp.float32
BF16 = jnp.bfloat16

D_MODEL = 1024
SB_HEADS = 8
SB_HEAD_DIM = 64
SB_WIDTH = SB_HEADS * SB_HEAD_DIM
CONV_WIDTH = 512
CONV_K = 3
XA_HEADS = 4
XA_HEAD_DIM = 128
XA_WIDTH = XA_HEADS * XA_HEAD_DIM
N_MEM = 256
D_FF = 4 * D_MODEL
N_BRANCH = 3
RMS_EPS = 1e-6
PAGE_SIZE = 128
IN_CHUNK = 512
LOG2E = math.log2(math.e)
SB_QSCALE = LOG2E / math.sqrt(SB_HEAD_DIM)
BIAS_TERMS = 3
MASKED_LOG2 =-1e30
XA_SCALE = 1.0 / math.sqrt(XA_HEAD_DIM)

V7X_VMEM_BYTES = 64 * 1024 * 1024
VMEM_LIMIT = V7X_VMEM_BYTES - 8 * 1024 * 1024

PROMPT_TILE = 512
SB_TILE = 256
SB_HEAD_GROUP = 8
DECODE_PAGES = 16
FF_CHUNK = 512


def _params(*sem):
    return pltpu.CompilerParams(dimension_semantics=sem, vmem_limit_bytes=VMEM_LIMIT)


def _resident(shape):
    nd = len(shape)
    return pl.BlockSpec(shape, lambda *_: (0,) * nd, pipeline_mode=pl.Buffered(1))


def _rms(x, g):
    return x * lax.rsqrt(jnp.mean(x * x, axis=-1, keepdims=True) + RMS_EPS) * g


def _dot(a, b):
    return jnp.dot(a, b, preferred_element_type=F32)


def _dot_nt(a, b):
    return lax.dot_general(a, b, (((1,), (1,)), ((), ())), preferred_element_type=F32)


def _softplus2(z2):
    return jnp.maximum(z2, jnp.log2(1.0 + jnp.exp2(jnp.minimum(z2, 64.0))))


def _bias_columns(b_sb):
    rest = b_sb.astype(F32) * LOG2E
    cols = []
    for _ in range(BIAS_TERMS):
        piece = rest.astype(BF16).astype(F32)
        cols.append(piece)
        rest = rest - piece
    pad = jnp.zeros((b_sb.shape[0], SB_HEAD_DIM - BIAS_TERMS), F32)
    return jnp.concatenate([jnp.stack(cols, axis=1), pad], axis=1)


def _neg_upper(n):
    s = lax.broadcasted_iota(jnp.int32, (n, n), 0)
    j = lax.broadcasted_iota(jnp.int32, (n, n), 1)
    return jnp.where(s > j, -1.0, 0.0).astype(BF16)


def _memkv_kernel(mem_ref, g_ref, w_ref, kv_ref, kvb_ref):
    mn = _rms(mem_ref[...], g_ref[...]).astype(BF16)
    kv = _dot(mn, w_ref[...])
    kv_ref[...] = kv
    kvb_ref[...] = kv.astype(BF16)


def _memkv(mem2d, g_mem, w_mem_kv_b):
    rows = mem2d.shape[0]
    return pl.pallas_call(
        _memkv_kernel,
        out_shape=(jax.ShapeDtypeStruct((rows, 2 * XA_WIDTH), F32),
                   jax.ShapeDtypeStruct((rows, 2 * XA_WIDTH), BF16)),
        compiler_params=pltpu.CompilerParams(vmem_limit_bytes=VMEM_LIMIT),
        name="memkv",
    )(mem2d, g_mem, w_mem_kv_b)


def _proj_prompt_kernel(x_ref, g_ref, win_ref, wconv_ref, mk_ref, mv_ref, kpad_ref,
                        q_ref, k_ref, v_ref, kt_ref, vt_ref, yconv_ref, cstate_ref, yxa_ref,
                        hist_ref):
    tm = x_ref.shape[1]
    i = pl.program_id(1)
    xn = _rms(x_ref[0], g_ref[...]).astype(BF16)

    def proj(c):
        return _dot(xn, win_ref[:, c * IN_CHUNK:(c + 1) * IN_CHUNK])

    q = proj(0) * SB_QSCALE
    k = proj(1)
    v = proj(2)
    lane = lax.broadcasted_iota(jnp.int32, (tm, SB_HEAD_DIM), 1)
    qpad = jnp.where(lane < BIAS_TERMS, 1.0, 0.0)
    for h in range(SB_HEADS):
        sl = slice(h * SB_HEAD_DIM, (h + 1) * SB_HEAD_DIM)
        kpad = jnp.broadcast_to(kpad_ref[h:h + 1, :], (tm, SB_HEAD_DIM))
        q_ref[0, h] = jnp.concatenate([q[:, sl], qpad], axis=1).astype(BF16)
        k_ref[0, h] = jnp.concatenate([k[:, sl], kpad], axis=1).astype(BF16)
        v_ref[0, h] = v[:, sl].astype(BF16)
    kt_ref[0] = k.T
    vt_ref[0] = v.T

    cb = proj(3)
    c = proj(4) * proj(5)

    @pl.when(i == 0)
    def _():
        hist_ref[0:8, :] = jnp.zeros((8, CONV_WIDTH), F32)

    hist_ref[8:8 + tm, :] = c
    c_m1 = hist_ref[pl.ds(7, tm), :]
    c_m2 = hist_ref[pl.ds(6, tm), :]
    w = wconv_ref[...]
    yconv_ref[0] = (cb * (c_m2 * w[0:1] + c_m1 * w[1:2] + c * w[2:3])).astype(BF16)
    tail = c[tm - (CONV_K - 1):tm, :]
    hist_ref[6:8, :] = tail
    cstate_ref[0] = tail

    xq = proj(6) * XA_SCALE
    for h in range(XA_HEADS):
        sl = slice(h * XA_HEAD_DIM, (h + 1) * XA_HEAD_DIM)
        s = _dot_nt(xq[:, sl].astype(BF16), mk_ref[0, :, sl])
        e = jnp.exp(s - jnp.max(s, axis=-1, keepdims=True))
        o = _dot(e.astype(BF16), mv_ref[0, :, sl]) / jnp.sum(e, axis=-1, keepdims=True)
        yxa_ref[0, :, sl] = o.astype(BF16)


def _proj_prompt(x, g, w_in_b, w_conv, mk_b, mv_b, kpad):
    nb, seq, _ = x.shape
    tm = PROMPT_TILE
    head_spec = pl.BlockSpec((1, SB_HEADS, tm, SB_HEAD_DIM), lambda b, i: (b, 0, i, 0))
    wide_spec = pl.BlockSpec((1, SB_HEADS, tm, 2 * SB_HEAD_DIM), lambda b, i: (b, 0, i, 0))
    t_spec = pl.BlockSpec((1, SB_WIDTH, tm), lambda b, i: (b, 0, i))
    row_spec = lambda w: pl.BlockSpec((1, tm, w), lambda b, i: (b, i, 0))
    mem_spec = pl.BlockSpec((1, N_MEM, XA_WIDTH), lambda b, i: (b, 0, 0))
    head_shape = jax.ShapeDtypeStruct((nb, SB_HEADS, seq, SB_HEAD_DIM), BF16)
    wide_shape = jax.ShapeDtypeStruct((nb, SB_HEADS, seq, 2 * SB_HEAD_DIM), BF16)
    t_shape = jax.ShapeDtypeStruct((nb, SB_WIDTH, seq), F32)
    return pl.pallas_call(
        _proj_prompt_kernel,
        grid=(nb, seq // tm),
        in_specs=[row_spec(D_MODEL), _resident(g.shape), _resident(w_in_b.shape),
                  _resident(w_conv.shape), mem_spec, mem_spec, _resident(kpad.shape)],
        out_specs=(wide_spec, wide_spec, head_spec, t_spec, t_spec, row_spec(CONV_WIDTH),
                   pl.BlockSpec((1, CONV_K - 1, CONV_WIDTH), lambda b, i: (b, 0, 0)),
                   row_spec(XA_WIDTH)),
        out_shape=(wide_shape, wide_shape, head_shape, t_shape, t_shape,
                   jax.ShapeDtypeStruct((nb, seq, CONV_WIDTH), BF16),
                   jax.ShapeDtypeStruct((nb, CONV_K - 1, CONV_WIDTH), F32),
                   jax.ShapeDtypeStruct((nb, seq, XA_WIDTH), BF16)),
        scratch_shapes=[pltpu.VMEM((tm + 8, CONV_WIDTH), F32)],
        compiler_params=_params("arbitrary", "arbitrary"),
        name="proj_prompt",
    )(x, g, w_in_b, w_conv, mk_b, mv_b, kpad)


def _sb_prompt_kernel(q_ref, k_ref, v_ref, negu_ref, o_ref, z_scr, p_scr, zmp_scr, w_scr, acc_scr,
                      carry_scr):
    ng, t = q_ref.shape[1], q_ref.shape[2]
    i = pl.program_id(2)
    negu = negu_ref[...]

    def rows(j):
        return pl.ds(pl.multiple_of(j * t, t), t)

    def logits(g, j):
        return _dot_nt(q_ref[0, g], k_ref[0, g, rows(j), :])

    def keep_and_beta(z):
        p = _softplus2(z)
        return p, z - p

    row = lax.broadcasted_iota(jnp.int32, (t, t), 0)
    col = lax.broadcasted_iota(jnp.int32, (t, t), 1)
    causal = col < row
    for g in range(ng):
        p, zmp = keep_and_beta(logits(g, i))
        p_scr[g] = jnp.where(causal, p, 0.0).astype(BF16)
        zmp_scr[g] = jnp.where(causal, zmp, MASKED_LOG2)
        z_scr[g] = logits(g, jnp.maximum(i - 1, 0))
        w_scr[g] = jnp.zeros((t, t), BF16)
        acc_scr[g] = jnp.zeros((t, SB_HEAD_DIM), F32)
        carry_scr[g] = jnp.zeros((t, 1), F32)

    def trip(n, _):
        j_prev = jnp.minimum(i - n + 1, i)
        j_next2 = jnp.maximum(i - n - 2, 0)
        for g in range(ng):
            acc_scr[g] += _dot(w_scr[g], v_ref[0, g, rows(j_prev), :])
            p_cur = p_scr[g]
            after = _dot(p_cur, negu)
            keep = after + carry_scr[g]
            w = jnp.exp2(zmp_scr[g] + keep)
            carry_scr[g] = keep[:, 0:1] - p_cur[:, 0:1].astype(F32)
            p, zmp = keep_and_beta(z_scr[g])
            z_scr[g] = logits(g, j_next2)
            w_scr[g] = w.astype(BF16)
            p_scr[g] = p.astype(BF16)
            zmp_scr[g] = zmp
        return 0

    lax.fori_loop(0, i + 1, trip, 0)
    heads = [acc_scr[g] + _dot(w_scr[g], v_ref[0, g, 0:t, :]) for g in range(ng)]
    o_ref[0] = jnp.concatenate(heads, axis=1).astype(BF16)


def _sb_prompt(q, k, v):
    nb, nh, seq, dh = v.shape
    t, ng = SB_TILE, SB_HEAD_GROUP
    kv_spec = lambda w: pl.BlockSpec((1, ng, seq, w), lambda b, h, i: (b, h, 0, 0),
                                     pipeline_mode=pl.Buffered(1))
    qo_spec = lambda w: pl.BlockSpec((1, ng, t, w), lambda b, h, i: (b, h, i, 0))
    return pl.pallas_call(
        _sb_prompt_kernel,
        grid=(nb, nh // ng, seq // t),
        in_specs=[qo_spec(q.shape[3]), kv_spec(k.shape[3]), kv_spec(dh), _resident((t, t))],
        out_specs=pl.BlockSpec((1, t, ng * dh), lambda b, h, i: (b, i, h)),
        out_shape=jax.ShapeDtypeStruct((nb, seq, nh * dh), BF16),
        scratch_shapes=[pltpu.VMEM((ng, t, t), F32),
                        pltpu.VMEM((ng, t, t), BF16), pltpu.VMEM((ng, t, t), F32),
                        pltpu.VMEM((ng, t, t), BF16), pltpu.VMEM((ng, t, dh), F32),
                        pltpu.VMEM((ng, t, 1), F32)],
        compiler_params=_params("arbitrary", "arbitrary", "arbitrary"),
        name="sb_prompt",
    )(q, k, v, _neg_upper(t))


def _merge_kernel(x_ref, ysb_ref, yconv_ref, yxa_ref, gpre_ref, wgate_ref, bgate_ref,
                  wsb_ref, wcv_ref, wxa_ref, wo_ref, gpost_ref, h_ref):
    x = x_ref[0]
    xn = _rms(x, gpre_ref[...]).astype(BF16)

    def gate(br):
        sl = slice(br * D_MODEL, (br + 1) * D_MODEL)
        return 1.0 / (1.0 + jnp.exp(-(_dot(xn, wgate_ref[:, sl]) + bgate_ref[:, sl])))

    m = gate(0) * _dot(ysb_ref[0], wsb_ref[...])
    m = m + gate(1) * _dot(yconv_ref[0], wcv_ref[...])
    m = m + gate(2) * _dot(yxa_ref[0], wxa_ref[...])
    mo = _dot(m.astype(BF16), wo_ref[...])
    h_ref[0] = x + _rms(mo, gpost_ref[...])


def _merge(x, ysb, yconv, yxa, g_pre, w_gate_b, b_gate, w_sb_b, w_cv_b, w_xa_b, w_o_b, g_post, tm):
    nb, seq, _ = x.shape
    row_spec = lambda w: pl.BlockSpec((1, tm, w), lambda b, i: (b, i, 0))
    consts = (g_pre, w_gate_b, b_gate, w_sb_b, w_cv_b, w_xa_b, w_o_b, g_post)
    return pl.pallas_call(
        _merge_kernel,
        grid=(nb, seq // tm),
        in_specs=[row_spec(D_MODEL), row_spec(SB_WIDTH), row_spec(CONV_WIDTH), row_spec(XA_WIDTH)]
                 + [_resident(c.shape) for c in consts],
        out_specs=row_spec(D_MODEL),
        out_shape=jax.ShapeDtypeStruct((nb, seq, D_MODEL), F32),
        compiler_params=_params("arbitrary", "arbitrary"),
        name="merge",
    )(x, ysb, yconv, yxa, *consts)


def _ffn_kernel(h_ref, gpre_ref, wup_ref, wdown_ref, gpost_ref, o_ref):
    h = h_ref[0]
    hn = _rms(h, gpre_ref[...]).astype(BF16)
    f = jnp.zeros(h.shape, F32)
    for c in range(D_FF // FF_CHUNK):
        sl = slice(c * FF_CHUNK, (c + 1) * FF_CHUNK)
        a = jnp.maximum(_dot(hn, wup_ref[:, sl]), 0.0)
        f = f + _dot((a * a).astype(BF16), wdown_ref[sl, :])
    o_ref[0] = h + _rms(f, gpost_ref[...])


def _ffn(h, g_pre, w_up_b, w_down_b, g_post, tm):
    nb, seq, _ = h.shape
    row_spec = pl.BlockSpec((1, tm, D_MODEL), lambda b, i: (b, i, 0))
    consts = (g_pre, w_up_b, w_down_b, g_post)
    return pl.pallas_call(
        _ffn_kernel,
        grid=(nb, seq // tm),
        in_specs=[row_spec] + [_resident(c.shape) for c in consts],
        out_specs=row_spec,
        out_shape=jax.ShapeDtypeStruct((nb, seq, D_MODEL), F32),
        compiler_params=_params("arbitrary", "arbitrary"),
        name="ffn",
    )(h, *consts)


def _proj_sample_kernel(x_ref, g_ref, win_ref, wconv_ref, s0_ref, s1_ref,
                        q_ref, k_ref, v_ref, yconv_ref, c_ref, xq_ref):
    xn = _rms(x_ref[...], g_ref[...]).astype(BF16)

    def proj(c):
        return _dot(xn, win_ref[:, c * IN_CHUNK:(c + 1) * IN_CHUNK])

    q_ref[...] = proj(0) * SB_QSCALE
    k_ref[...] = proj(1)
    v_ref[...] = proj(2)
    cb = proj(3)
    c = proj(4) * proj(5)
    w = wconv_ref[...]
    yconv_ref[...] = (cb * (s0_ref[...] * w[0:1] + s1_ref[...] * w[1:2] + c * w[2:3])).astype(BF16)
    c_ref[...] = c
    xq_ref[...] = proj(6) * XA_SCALE


def _proj_sample(x2d, g, w_in_b, w_conv, s0, s1):
    n = x2d.shape[0]
    wide = jax.ShapeDtypeStruct((n, IN_CHUNK), F32)
    return pl.pallas_call(
        _proj_sample_kernel,
        out_shape=(wide, wide, wide, jax.ShapeDtypeStruct((n, CONV_WIDTH), BF16), wide, wide),
        compiler_params=pltpu.CompilerParams(vmem_limit_bytes=VMEM_LIMIT),
        name="proj_sample",
    )(x2d, g, w_in_b, w_conv, s0, s1)


def _xattn_sample_kernel(xq_ref, mk_ref, mv_ref, o_ref):
    xq = xq_ref[0]
    for h in range(XA_HEADS):
        sl = slice(h * XA_HEAD_DIM, (h + 1) * XA_HEAD_DIM)
        qh = jnp.broadcast_to(xq[:, sl], (8, XA_HEAD_DIM)).astype(BF16)
        s = _dot_nt(qh, mk_ref[0, :, sl].astype(BF16))
        e = jnp.exp(s - jnp.max(s, axis=-1, keepdims=True))
        o = _dot(e.astype(BF16), mv_ref[0, :, sl].astype(BF16)) / jnp.sum(e, axis=-1, keepdims=True)
        o_ref[0, :, sl] = o[0:1].astype(BF16)


def _xattn_sample(xq, mk, mv):
    n = xq.shape[0]
    q_spec = pl.BlockSpec((1, 1, XA_WIDTH), lambda b: (b, 0, 0))
    m_spec = pl.BlockSpec((1, N_MEM, XA_WIDTH), lambda b: (b, 0, 0))
    return pl.pallas_call(
        _xattn_sample_kernel,
        grid=(n,),
        in_specs=[q_spec, m_spec, m_spec],
        out_specs=q_spec,
        out_shape=jax.ShapeDtypeStruct((n, 1, XA_WIDTH), BF16),
        compiler_params=_params("arbitrary"),
        name="xattn_sample",
    )(xq.reshape(n, 1, XA_WIDTH), mk, mv)


def _sb_decode_kernel(pt_ref, q_ref, bias_ref, negu_ref, *refs):
    del pt_ref
    n = DECODE_PAGES
    k_refs, v_refs = refs[:n], refs[n:2 * n]
    o_ref, acc_ref, carry_ref, w_ref = refs[2 * n:]
    g = pl.program_id(1)

    @pl.when(g == 0)
    def _():
        acc_ref[...] = jnp.zeros(acc_ref.shape, F32)
        carry_ref[...] = jnp.zeros(carry_ref.shape, F32)
        w_ref[...] = jnp.zeros(w_ref.shape, F32)

    for h in range(SB_HEADS):
        acc = acc_ref[h]
        for p in range(n):
            acc = acc + w_ref[pl.ds(p * SB_HEADS + h, 1), :] * v_refs[p][0, h]
        acc_ref[h] = acc

    qb = q_ref[0]
    z = jnp.concatenate([jnp.sum(qb * k_refs[p][0], axis=1) for p in range(n)], axis=0)
    z = z + bias_ref[...]
    p_all = _softplus2(z)
    negu = negu_ref[...]
    p_hi = p_all.astype(BF16)
    rest = p_all - p_hi.astype(F32)
    p_mid = rest.astype(BF16)
    p_lo = (rest - p_mid.astype(F32)).astype(BF16)
    after = _dot(p_hi, negu) + _dot(p_mid, negu) + _dot(p_lo, negu)
    total = after[:, 0:1] - p_all[:, 0:1]
    carry = carry_ref[...]
    for p in range(n):
        sl = slice(p * SB_HEADS, (p + 1) * SB_HEADS)
        w_ref[sl, :] = jnp.exp2((z[sl] - p_all[sl]) + (after[sl] + carry))
        carry = carry + total[sl]
    carry_ref[...] = carry

    @pl.when(g == pl.num_programs(1) - 1)
    def _():
        o_ref[0] = jnp.sum(acc_ref[...], axis=2, keepdims=True)


def _sb_decode(page_table, q, b_sb, kt_pages, vt_pages):
    nseq, n_pages = page_table.shape
    n = DECODE_PAGES
    bias = jnp.tile(b_sb.reshape(SB_HEADS, 1) * LOG2E, (n, 1))

    n_groups = n_pages // n

    def page_spec(p, lag):
        def index(b, g, pt):
            group = jnp.clip(g - lag, 0, n_groups - 1)
            return (pt[b, n_pages - 1 - (group * n + p)], 0, 0, 0)
        return pl.BlockSpec((1, SB_HEADS, SB_HEAD_DIM, PAGE_SIZE), index)

    q_spec = pl.BlockSpec((1, SB_HEADS, SB_HEAD_DIM, 1), lambda b, g, pt: (b, 0, 0, 0))
    grid_spec = pltpu.PrefetchScalarGridSpec(
        num_scalar_prefetch=1,
        grid=(nseq, n_groups + 1),
        in_specs=[q_spec,
                  pl.BlockSpec((n * SB_HEADS, 1), lambda b, g, pt: (0, 0)),
                  pl.BlockSpec((PAGE_SIZE, PAGE_SIZE), lambda b, g, pt: (0, 0))]
                 + [page_spec(p, 0) for p in range(n)] + [page_spec(p, 1) for p in range(n)],
        out_specs=q_spec,
        scratch_shapes=[pltpu.VMEM((SB_HEADS, SB_HEAD_DIM, PAGE_SIZE), F32),
                        pltpu.VMEM((SB_HEADS, 1), F32),
                        pltpu.VMEM((n * SB_HEADS, PAGE_SIZE), F32)],
    )
    return pl.pallas_call(
        _sb_decode_kernel,
        grid_spec=grid_spec,
        out_shape=jax.ShapeDtypeStruct((nseq, SB_HEADS, SB_HEAD_DIM, 1), F32),
        compiler_params=_params("arbitrary", "arbitrary"),
        name="sb_decode",
    )(page_table, q.reshape(nseq, SB_HEADS, SB_HEAD_DIM, 1), bias, _neg_upper(PAGE_SIZE),
      *([kt_pages] * n), *([vt_pages] * n))


def kernel(x_prompt, x_sample, mem_prompt, cache_k_pages, cache_v_pages, page_table, cache_mem_k,
           cache_mem_v, state_conv, g_mix_pre, w_in, b_sb, w_conv, g_mem, w_mem_kv, w_gate, b_gate,
           w_sb_o, w_conv_o, w_xa_o, w_o, g_mix_post, g_ffn_pre, w_up, w_down, g_ffn_post):
    depth = w_in.shape[0]
    assert depth == 1, "single-layer step"
    nb, seq, _ = x_prompt.shape
    nseq = x_sample.shape[0]
    bf = lambda a: a[0].astype(BF16)
    w_in_b, w_gate_b, w_mem_b = bf(w_in), bf(w_gate), bf(w_mem_kv)
    w_sb_b, w_cv_b, w_xa_b, w_o_b = bf(w_sb_o), bf(w_conv_o), bf(w_xa_o), bf(w_o)
    w_up_b, w_down_b = bf(w_up), bf(w_down)
    tail = (g_mix_pre, w_gate_b, b_gate, w_sb_b, w_cv_b, w_xa_b, w_o_b, g_mix_post)
    ffn_w = (g_ffn_pre, w_up_b, w_down_b, g_ffn_post)

    kv, kv_b = _memkv(mem_prompt.reshape(nb * N_MEM, D_MODEL), g_mem, w_mem_b)
    mk = kv[:, :XA_WIDTH].reshape(1, nb, N_MEM, XA_HEADS, XA_HEAD_DIM)
    mv = kv[:, XA_WIDTH:].reshape(1, nb, N_MEM, XA_HEADS, XA_HEAD_DIM)
    mk_b = kv_b[:, :XA_WIDTH].reshape(nb, N_MEM, XA_WIDTH)
    mv_b = kv_b[:, XA_WIDTH:].reshape(nb, N_MEM, XA_WIDTH)
    q_p, k_p, v_p, kt, vt, yconv_p, cstate_p, yxa_p = _proj_prompt(
        x_prompt, g_mix_pre, w_in_b, w_conv[0], mk_b, mv_b, _bias_columns(b_sb[0]))
    ysb_p = _sb_prompt(q_p, k_p, v_p)
    h_p = _merge(x_prompt, ysb_p, yconv_p, yxa_p, *tail, tm=PROMPT_TILE)
    y_p = _ffn(h_p, *ffn_w, tm=PROMPT_TILE)
    to_cache = lambda a: jnp.transpose(
        a.reshape(nb, SB_HEADS, SB_HEAD_DIM, seq), (0, 3, 1, 2))[None]
    k_prompt, v_prompt = to_cache(kt), to_cache(vt)

    xs = x_sample.reshape(nseq, D_MODEL)
    q_s, k_s, v_s, yconv_s, c_s, xq_s = _proj_sample(
        xs, g_mix_pre, w_in_b, w_conv[0], state_conv[0, :, 0, :], state_conv[0, :, 1, :])
    kt_pages = jnp.transpose(cache_k_pages[0], (0, 2, 3, 1))
    vt_pages = jnp.transpose(cache_v_pages[0], (0, 2, 3, 1))
    ysb_s = _sb_decode(page_table, q_s, b_sb[0], kt_pages, vt_pages)
    ysb_s = ysb_s.reshape(1, nseq, SB_WIDTH).astype(BF16)
    yxa_s = _xattn_sample(xq_s, cache_mem_k[0].reshape(nseq, N_MEM, XA_WIDTH),
                          cache_mem_v[0].reshape(nseq, N_MEM, XA_WIDTH))
    h_s = _merge(xs[None], ysb_s, yconv_s[None], yxa_s.reshape(1, nseq, XA_WIDTH), *tail, tm=nseq)
    y_s = _ffn(h_s, *ffn_w, tm=nseq)
    conv_sample = jnp.stack([state_conv[0, :, 1, :], c_s], axis=1)[None]
    to_tok = lambda a: a.reshape(1, nseq, 1, SB_HEADS, SB_HEAD_DIM)

    return (y_p, y_s.reshape(nseq, 1, D_MODEL), k_prompt, v_prompt, cstate_p[None], mk, mv,
            to_tok(k_s), to_tok(v_s), conv_sample)
```

```python
import functools
import math

import jax
import jax.numpy as jnp
from jax import lax
from jax.experimental import pallas as pl
from jax.experimental.pallas import tpu as pltpu

F32 = jnp.float32
BF16 = jnp.bfloat16

D_MODEL = 1024
SB_HEADS = 8
SB_HEAD_DIM = 64
SB_WIDTH = SB_HEADS * SB_HEAD_DIM
CONV_WIDTH = 512
CONV_K = 3
XA_HEADS = 4
XA_HEAD_DIM = 128
XA_WIDTH = XA_HEADS * XA_HEAD_DIM
N_MEM = 256
D_FF = 4 * D_MODEL
N_BRANCH = 3
RMS_EPS = 1e-6
PAGE_SIZE = 128
IN_CHUNK = 512
LOG2E = math.log2(math.e)
SB_QSCALE = LOG2E / math.sqrt(SB_HEAD_DIM)
BIAS_TERMS = 3
MASKED_LOG2 =-1e30
XA_SCALE = 1.0 / math.sqrt(XA_HEAD_DIM)

V7X_VMEM_BYTES = 64 * 1024 * 1024
VMEM_LIMIT = V7X_VMEM_BYTES - 8 * 1024 * 1024

PROMPT_TILE = 512
SB_TILE = 256
SB_HEAD_GROUP = 8
DECODE_PAGES = 16
FF_CHUNK = 512
FF_CHUNKS = D_FF // FF_CHUNK
FFN_DECODE_PAGES = 8
HOSTED_PAGES = FF_CHUNKS * FFN_DECODE_PAGES
DECODE_RING = 3


def _params(*sem):
    return pltpu.CompilerParams(dimension_semantics=sem, vmem_limit_bytes=VMEM_LIMIT)


def _resident(shape):
    nd = len(shape)
    return pl.BlockSpec(shape, lambda *_: (0,) * nd, pipeline_mode=pl.Buffered(1))


def _rms(x, g):
    return x * lax.rsqrt(jnp.mean(x * x, axis=-1, keepdims=True) + RMS_EPS) * g


def _dot(a, b):
    return jnp.dot(a, b, preferred_element_type=F32)


def _dot_nt(a, b):
    return lax.dot_general(a, b, (((1,), (1,)), ((), ())), preferred_element_type=F32)


def _softplus2(z2):
    return jnp.maximum(z2, jnp.log2(1.0 + jnp.exp2(jnp.minimum(z2, 64.0))))


def _bias_columns(b_sb):
    rest = b_sb.astype(F32) * LOG2E
    cols = []
    for _ in range(BIAS_TERMS):
        piece = rest.astype(BF16).astype(F32)
        cols.append(piece)
        rest = rest - piece
    pad = jnp.zeros((b_sb.shape[0], SB_HEAD_DIM - BIAS_TERMS), F32)
    return jnp.concatenate([jnp.stack(cols, axis=1), pad], axis=1)


def _neg_upper(n):
    s = lax.broadcasted_iota(jnp.int32, (n, n), 0)
    j = lax.broadcasted_iota(jnp.int32, (n, n), 1)
    return jnp.where(s > j, -1.0, 0.0).astype(BF16)


def _memkv_kernel(mem_ref, g_ref, w_ref, kv_ref, kvb_ref):
    mn = _rms(mem_ref[...], g_ref[...]).astype(BF16)
    kv = _dot(mn, w_ref[...])
    kv_ref[...] = kv
    kvb_ref[...] = kv.astype(BF16)


def _memkv(mem2d, g_mem, w_mem_kv_b):
    rows = mem2d.shape[0]
    return pl.pallas_call(
        _memkv_kernel,
        out_shape=(jax.ShapeDtypeStruct((rows, 2 * XA_WIDTH), F32),
                   jax.ShapeDtypeStruct((rows, 2 * XA_WIDTH), BF16)),
        compiler_params=pltpu.CompilerParams(vmem_limit_bytes=VMEM_LIMIT),
        name="memkv",
    )(mem2d, g_mem, w_mem_kv_b)


def _proj_prompt_kernel(x_ref, g_ref, win_ref, wconv_ref, mk_ref, mv_ref, kpad_ref,
                        q_ref, k_ref, v_ref, kt_ref, vt_ref, yconv_ref, cstate_ref, yxa_ref,
                        hist_ref):
    tm = x_ref.shape[1]
    i = pl.program_id(1)
    xn = _rms(x_ref[0], g_ref[...]).astype(BF16)

    def proj(c):
        return _dot(xn, win_ref[:, c * IN_CHUNK:(c + 1) * IN_CHUNK])

    q = proj(0) * SB_QSCALE
    k = proj(1)
    v = proj(2)
    lane = lax.broadcasted_iota(jnp.int32, (tm, SB_HEAD_DIM), 1)
    qpad = jnp.where(lane < BIAS_TERMS, 1.0, 0.0)
    for h in range(SB_HEADS):
        sl = slice(h * SB_HEAD_DIM, (h + 1) * SB_HEAD_DIM)
        kpad = jnp.broadcast_to(kpad_ref[h:h + 1, :], (tm, SB_HEAD_DIM))
        q_ref[0, h] = jnp.concatenate([q[:, sl], qpad], axis=1).astype(BF16)
        k_ref[0, h] = jnp.concatenate([k[:, sl], kpad], axis=1).astype(BF16)
        v_ref[0, h] = v[:, sl].astype(BF16)
    kt_ref[0] = k.T
    vt_ref[0] = v.T

    cb = proj(3)
    c = proj(4) * proj(5)

    @pl.when(i == 0)
    def _():
        hist_ref[0:8, :] = jnp.zeros((8, CONV_WIDTH), F32)

    hist_ref[8:8 + tm, :] = c
    c_m1 = hist_ref[pl.ds(7, tm), :]
    c_m2 = hist_ref[pl.ds(6, tm), :]
    w = wconv_ref[...]
    yconv_ref[0] = (cb * (c_m2 * w[0:1] + c_m1 * w[1:2] + c * w[2:3])).astype(BF16)
    tail = c[tm - (CONV_K - 1):tm, :]
    hist_ref[6:8, :] = tail
    cstate_ref[0] = tail

    xq = proj(6) * XA_SCALE
    for h in range(XA_HEADS):
        sl = slice(h * XA_HEAD_DIM, (h + 1) * XA_HEAD_DIM)
        s = _dot_nt(xq[:, sl].astype(BF16), mk_ref[0, :, sl])
        e = jnp.exp(s - jnp.max(s, axis=-1, keepdims=True))
        o = _dot(e.astype(BF16), mv_ref[0, :, sl]) / jnp.sum(e, axis=-1, keepdims=True)
        yxa_ref[0, :, sl] = o.astype(BF16)


def _proj_prompt(x, g, w_in_b, w_conv, mk_b, mv_b, kpad):
    nb, seq, _ = x.shape
    tm = PROMPT_TILE
    head_spec = pl.BlockSpec((1, SB_HEADS, tm, SB_HEAD_DIM), lambda b, i: (b, 0, i, 0))
    wide_spec = pl.BlockSpec((1, SB_HEADS, tm, 2 * SB_HEAD_DIM), lambda b, i: (b, 0, i, 0))
    t_spec = pl.BlockSpec((1, SB_WIDTH, tm), lambda b, i: (b, 0, i))
    row_spec = lambda w: pl.BlockSpec((1, tm, w), lambda b, i: (b, i, 0))
    mem_spec = pl.BlockSpec((1, N_MEM, XA_WIDTH), lambda b, i: (b, 0, 0))
    head_shape = jax.ShapeDtypeStruct((nb, SB_HEADS, seq, SB_HEAD_DIM), BF16)
    wide_shape = jax.ShapeDtypeStruct((nb, SB_HEADS, seq, 2 * SB_HEAD_DIM), BF16)
    t_shape = jax.ShapeDtypeStruct((nb, SB_WIDTH, seq), F32)
    return pl.pallas_call(
        _proj_prompt_kernel,
        grid=(nb, seq // tm),
        in_specs=[row_spec(D_MODEL), _resident(g.shape), _resident(w_in_b.shape),
                  _resident(w_conv.shape), mem_spec, mem_spec, _resident(kpad.shape)],
        out_specs=(wide_spec, wide_spec, head_spec, t_spec, t_spec, row_spec(CONV_WIDTH),
                   pl.BlockSpec((1, CONV_K - 1, CONV_WIDTH), lambda b, i: (b, 0, 0)),
                   row_spec(XA_WIDTH)),
        out_shape=(wide_shape, wide_shape, head_shape, t_shape, t_shape,
                   jax.ShapeDtypeStruct((nb, seq, CONV_WIDTH), BF16),
                   jax.ShapeDtypeStruct((nb, CONV_K - 1, CONV_WIDTH), F32),
                   jax.ShapeDtypeStruct((nb, seq, XA_WIDTH), BF16)),
        scratch_shapes=[pltpu.VMEM((tm + 8, CONV_WIDTH), F32)],
        compiler_params=_params("arbitrary", "arbitrary"),
        name="proj_prompt",
    )(x, g, w_in_b, w_conv, mk_b, mv_b, kpad)


def _sb_prompt_kernel(q_ref, k_ref, v_ref, negu_ref, o_ref, z_scr, p_scr, zmp_scr, w_scr, acc_scr,
                      carry_scr):
    ng, t = q_ref.shape[1], q_ref.shape[2]
    i = pl.program_id(2)
    negu = negu_ref[...]

    def rows(j):
        return pl.ds(pl.multiple_of(j * t, t), t)

    def logits(g, j):
        return _dot_nt(q_ref[0, g], k_ref[0, g, rows(j), :])

    def keep_and_beta(z):
        p = _softplus2(z)
        return p, z - p

    row = lax.broadcasted_iota(jnp.int32, (t, t), 0)
    col = lax.broadcasted_iota(jnp.int32, (t, t), 1)
    causal = col < row
    for g in range(ng):
        p, zmp = keep_and_beta(logits(g, i))
        p_scr[g] = jnp.where(causal, p, 0.0).astype(BF16)
        zmp_scr[g] = jnp.where(causal, zmp, MASKED_LOG2)
        z_scr[g] = logits(g, jnp.maximum(i - 1, 0))
        w_scr[g] = jnp.zeros((t, t), BF16)
        acc_scr[g] = jnp.zeros((t, SB_HEAD_DIM), F32)
        carry_scr[g] = jnp.zeros((t, 1), F32)

    def trip(n, _):
        j_prev = jnp.minimum(i - n + 1, i)
        j_next2 = jnp.maximum(i - n - 2, 0)
        for g in range(ng):
            acc_scr[g] += _dot(w_scr[g], v_ref[0, g, rows(j_prev), :])
            p_cur = p_scr[g]
            after = _dot(p_cur, negu)
            keep = after + carry_scr[g]
            w = jnp.exp2(zmp_scr[g] + keep)
            carry_scr[g] = keep[:, 0:1] - p_cur[:, 0:1].astype(F32)
            p, zmp = keep_and_beta(z_scr[g])
            z_scr[g] = logits(g, j_next2)
            w_scr[g] = w.astype(BF16)
            p_scr[g] = p.astype(BF16)
            zmp_scr[g] = zmp
        return 0

    lax.fori_loop(0, i + 1, trip, 0)
    heads = [acc_scr[g] + _dot(w_scr[g], v_ref[0, g, 0:t, :]) for g in range(ng)]
    o_ref[0] = jnp.concatenate(heads, axis=1).astype(BF16)


def _sb_prompt(q, k, v):
    nb, nh, seq, dh = v.shape
    t, ng = SB_TILE, SB_HEAD_GROUP
    kv_spec = lambda w: pl.BlockSpec((1, ng, seq, w), lambda b, h, i: (b, h, 0, 0),
                                     pipeline_mode=pl.Buffered(1))
    qo_spec = lambda w: pl.BlockSpec((1, ng, t, w), lambda b, h, i: (b, h, i, 0))
    return pl.pallas_call(
        _sb_prompt_kernel,
        grid=(nb, nh // ng, seq // t),
        in_specs=[qo_spec(q.shape[3]), kv_spec(k.shape[3]), kv_spec(dh), _resident((t, t))],
        out_specs=pl.BlockSpec((1, t, ng * dh), lambda b, h, i: (b, i, h)),
        out_shape=jax.ShapeDtypeStruct((nb, seq, nh * dh), BF16),
        scratch_shapes=[pltpu.VMEM((ng, t, t), F32),
                        pltpu.VMEM((ng, t, t), BF16), pltpu.VMEM((ng, t, t), F32),
                        pltpu.VMEM((ng, t, t), BF16), pltpu.VMEM((ng, t, dh), F32),
                        pltpu.VMEM((ng, t, 1), F32)],
        compiler_params=_params("arbitrary", "arbitrary", "arbitrary"),
        name="sb_prompt",
    )(q, k, v, _neg_upper(t))


def _merge_kernel(x_ref, ysb_ref, yconv_ref, yxa_ref, gpre_ref, wgate_ref, bgate_ref,
                  wsb_ref, wcv_ref, wxa_ref, wo_ref, gpost_ref, h_ref):
    x = x_ref[0]
    xn = _rms(x, gpre_ref[...]).astype(BF16)

    def gate(br):
        sl = slice(br * D_MODEL, (br + 1) * D_MODEL)
        return 1.0 / (1.0 + jnp.exp(-(_dot(xn, wgate_ref[:, sl]) + bgate_ref[:, sl])))

    m = gate(0) * _dot(ysb_ref[0], wsb_ref[...])
    m = m + gate(1) * _dot(yconv_ref[0], wcv_ref[...])
    m = m + gate(2) * _dot(yxa_ref[0], wxa_ref[...])
    mo = _dot(m.astype(BF16), wo_ref[...])
    h_ref[0] = x + _rms(mo, gpost_ref[...])


def _merge(x, ysb, yconv, yxa, g_pre, w_gate_b, b_gate, w_sb_b, w_cv_b, w_xa_b, w_o_b, g_post, tm):
    nb, seq, _ = x.shape
    row_spec = lambda w: pl.BlockSpec((1, tm, w), lambda b, i: (b, i, 0))
    consts = (g_pre, w_gate_b, b_gate, w_sb_b, w_cv_b, w_xa_b, w_o_b, g_post)
    return pl.pallas_call(
        _merge_kernel,
        grid=(nb, seq // tm),
        in_specs=[row_spec(D_MODEL), row_spec(SB_WIDTH), row_spec(CONV_WIDTH), row_spec(XA_WIDTH)]
                 + [_resident(c.shape) for c in consts],
        out_specs=row_spec(D_MODEL),
        out_shape=jax.ShapeDtypeStruct((nb, seq, D_MODEL), F32),
        compiler_params=_params("arbitrary", "arbitrary"),
        name="merge",
    )(x, ysb, yconv, yxa, *consts)


def _ffn_tile(h_ref, gpre_ref, wup_ref, wdown_ref, gpost_ref, o_ref, before_chunk=None,
              after_chunk=None):
    h = h_ref[0]
    hn = _rms(h, gpre_ref[...]).astype(BF16)
    f = jnp.zeros(h.shape, F32)
    for c in range(FF_CHUNKS):
        if before_chunk is not None:
            before_chunk(c)
        sl = slice(c * FF_CHUNK, (c + 1) * FF_CHUNK)
        a = jnp.maximum(_dot(hn, wup_ref[:, sl]), 0.0)
        f = f + _dot((a * a).astype(BF16), wdown_ref[sl, :])
        if after_chunk is not None:
            after_chunk(c)
    o_ref[0] = h + _rms(f, gpost_ref[...])


def _ffn_kernel(h_ref, gpre_ref, wup_ref, wdown_ref, gpost_ref, o_ref):
    _ffn_tile(h_ref, gpre_ref, wup_ref, wdown_ref, gpost_ref, o_ref)


def _ffn(h, g_pre, w_up_b, w_down_b, g_post, tm):
    nb, seq, _ = h.shape
    row_spec = pl.BlockSpec((1, tm, D_MODEL), lambda b, i: (b, i, 0))
    consts = (g_pre, w_up_b, w_down_b, g_post)
    return pl.pallas_call(
        _ffn_kernel,
        grid=(nb, seq // tm),
        in_specs=[row_spec] + [_resident(c.shape) for c in consts],
        out_specs=row_spec,
        out_shape=jax.ShapeDtypeStruct((nb, seq, D_MODEL), F32),
        compiler_params=_params("arbitrary", "arbitrary"),
        name="ffn",
    )(h, *consts)


def _ffn_decode_kernel(pt_ref, h_ref, gpre_ref, wup_ref, wdown_ref, gpost_ref, q_ref, bias_ref,
                       negu_ref, kt_hbm, vt_hbm, o_ref, ysb_ref, kbuf, vbuf, sems, acc_ref,
                       carry_ref):
    n = FFN_DECODE_PAGES
    n_pages = pt_ref.shape[1]
    steps_per_seq = n_pages // HOSTED_PAGES
    step = pl.program_id(0) * pl.num_programs(1) + pl.program_id(1)
    seq = step // steps_per_seq
    part = step % steps_per_seq

    def copies(c):
        slot = c % DECODE_RING
        out = []
        for p in range(n):
            page = pt_ref[seq, n_pages - 1 - (part * HOSTED_PAGES + c * n + p)]
            out.append(pltpu.make_async_copy(kt_hbm.at[page], kbuf.at[slot, p], sems.at[0, slot, p]))
            out.append(pltpu.make_async_copy(vt_hbm.at[page], vbuf.at[slot, p], sems.at[1, slot, p]))
        return out

    @pl.when(part == 0)
    def _():
        acc_ref[...] = jnp.zeros(acc_ref.shape, F32)
        carry_ref[...] = jnp.zeros(carry_ref.shape, F32)

    def before_chunk(c):
        if c == 0:
            for ahead in range(min(DECODE_RING, FF_CHUNKS)):
                for cp in copies(ahead):
                    cp.start()

    def after_chunk(c):
        slot = c % DECODE_RING
        for cp in copies(c):
            cp.wait()
        ws, carry = _decode_weights(q_ref[0], bias_ref[...], negu_ref[...],
                                    [kbuf[slot, p] for p in range(n)], carry_ref[...])
        carry_ref[...] = carry
        for h in range(SB_HEADS):
            acc = acc_ref[h]
            for p in range(n):
                acc = acc + ws[p][h:h + 1, :] * vbuf[slot, p, h]
            acc_ref[h] = acc
        if c + DECODE_RING < FF_CHUNKS:
            for cp in copies(c + DECODE_RING):
                cp.start()

    _ffn_tile(h_ref, gpre_ref, wup_ref, wdown_ref, gpost_ref, o_ref, before_chunk, after_chunk)

    @pl.when(part == steps_per_seq - 1)
    def _():
        ysb_ref[0] = jnp.sum(acc_ref[...], axis=2, keepdims=True)


def _ffn_decode(h, g_pre, w_up_b, w_down_b, g_post, tm, page_table, q, b_sb, kt_pages, vt_pages):
    nb, seq, _ = h.shape
    n_pages = page_table.shape[1]
    n_steps = nb * (seq // tm)
    n_hosted = n_steps * HOSTED_PAGES // n_pages
    steps_per_seq = n_pages // HOSTED_PAGES
    n = FFN_DECODE_PAGES
    row_spec = pl.BlockSpec((1, tm, D_MODEL), lambda b, i, pt: (b, i, 0))
    seq_spec = pl.BlockSpec((1, SB_HEADS, SB_HEAD_DIM, 1),
                            lambda b, i, pt: ((b * (seq // tm) + i) // steps_per_seq, 0, 0, 0))
    consts = (g_pre, w_up_b, w_down_b, g_post)
    bias = jnp.tile(b_sb.reshape(SB_HEADS, 1) * LOG2E, (n, 1))
    negu = _neg_upper(PAGE_SIZE)
    page_shape = (SB_HEADS, SB_HEAD_DIM, PAGE_SIZE)
    grid_spec = pltpu.PrefetchScalarGridSpec(
        num_scalar_prefetch=1,
        grid=(nb, seq // tm),
        in_specs=[row_spec] + [_resident(c.shape) for c in consts]
                 + [seq_spec, _resident(bias.shape), _resident(negu.shape),
                    pl.BlockSpec(memory_space=pl.ANY), pl.BlockSpec(memory_space=pl.ANY)],
        out_specs=(row_spec, seq_spec),
        scratch_shapes=[pltpu.VMEM((DECODE_RING, n) + page_shape, F32),
                        pltpu.VMEM((DECODE_RING, n) + page_shape, F32),
                        pltpu.SemaphoreType.DMA((2, DECODE_RING, n)),
                        pltpu.VMEM(page_shape, F32),
                        pltpu.VMEM((SB_HEADS, 1), F32)],
    )
    return pl.pallas_call(
        _ffn_decode_kernel,
        grid_spec=grid_spec,
        out_shape=(jax.ShapeDtypeStruct((nb, seq, D_MODEL), F32),
                   jax.ShapeDtypeStruct((n_hosted, SB_HEADS, SB_HEAD_DIM, 1), F32)),
        compiler_params=_params("arbitrary", "arbitrary"),
        name="ffn_decode",
    )(page_table[:n_hosted], h, *consts,
      q[:n_hosted].reshape(n_hosted, SB_HEADS, SB_HEAD_DIM, 1), bias, negu, kt_pages, vt_pages)


def _proj_sample_kernel(x_ref, g_ref, win_ref, wconv_ref, s0_ref, s1_ref,
                        q_ref, k_ref, v_ref, yconv_ref, c_ref, xq_ref):
    xn = _rms(x_ref[...], g_ref[...]).astype(BF16)

    def proj(c):
        return _dot(xn, win_ref[:, c * IN_CHUNK:(c + 1) * IN_CHUNK])

    q_ref[...] = proj(0) * SB_QSCALE
    k_ref[...] = proj(1)
    v_ref[...] = proj(2)
    cb = proj(3)
    c = proj(4) * proj(5)
    w = wconv_ref[...]
    yconv_ref[...] = (cb * (s0_ref[...] * w[0:1] + s1_ref[...] * w[1:2] + c * w[2:3])).astype(BF16)
    c_ref[...] = c
    xq_ref[...] = proj(6) * XA_SCALE


def _proj_sample(x2d, g, w_in_b, w_conv, s0, s1):
    n = x2d.shape[0]
    wide = jax.ShapeDtypeStruct((n, IN_CHUNK), F32)
    return pl.pallas_call(
        _proj_sample_kernel,
        out_shape=(wide, wide, wide, jax.ShapeDtypeStruct((n, CONV_WIDTH), BF16), wide, wide),
        compiler_params=pltpu.CompilerParams(vmem_limit_bytes=VMEM_LIMIT),
        name="proj_sample",
    )(x2d, g, w_in_b, w_conv, s0, s1)


def _xattn_sample_kernel(xq_ref, mk_ref, mv_ref, o_ref):
    xq = xq_ref[0]
    for h in range(XA_HEADS):
        sl = slice(h * XA_HEAD_DIM, (h + 1) * XA_HEAD_DIM)
        qh = jnp.broadcast_to(xq[:, sl], (8, XA_HEAD_DIM)).astype(BF16)
        s = _dot_nt(qh, mk_ref[0, :, sl].astype(BF16))
        e = jnp.exp(s - jnp.max(s, axis=-1, keepdims=True))
        o = _dot(e.astype(BF16), mv_ref[0, :, sl].astype(BF16)) / jnp.sum(e, axis=-1, keepdims=True)
        o_ref[0, :, sl] = o[0:1].astype(BF16)


def _xattn_sample(xq, mk, mv):
    n = xq.shape[0]
    q_spec = pl.BlockSpec((1, 1, XA_WIDTH), lambda b: (b, 0, 0))
    m_spec = pl.BlockSpec((1, N_MEM, XA_WIDTH), lambda b: (b, 0, 0))
    return pl.pallas_call(
        _xattn_sample_kernel,
        grid=(n,),
        in_specs=[q_spec, m_spec, m_spec],
        out_specs=q_spec,
        out_shape=jax.ShapeDtypeStruct((n, 1, XA_WIDTH), BF16),
        compiler_params=_params("arbitrary"),
        name="xattn_sample",
    )(xq.reshape(n, 1, XA_WIDTH), mk, mv)


def _decode_weights(qb, bias, negu, k_pages, carry):
    z = jnp.concatenate([jnp.sum(qb * kp, axis=1) for kp in k_pages], axis=0) + bias
    p_all = _softplus2(z)
    p_hi = p_all.astype(BF16)
    rest = p_all - p_hi.astype(F32)
    p_mid = rest.astype(BF16)
    p_lo = (rest - p_mid.astype(F32)).astype(BF16)
    after = _dot(p_hi, negu) + _dot(p_mid, negu) + _dot(p_lo, negu)
    total = after[:, 0:1] - p_all[:, 0:1]
    ws = []
    for p in range(len(k_pages)):
        sl = slice(p * SB_HEADS, (p + 1) * SB_HEADS)
        ws.append(jnp.exp2((z[sl] - p_all[sl]) + (after[sl] + carry)))
        carry = carry + total[sl]
    return ws, carry


def _sb_decode_kernel(pt_ref, q_ref, bias_ref, negu_ref, *refs):
    del pt_ref
    n = DECODE_PAGES
    k_refs, v_refs = refs[:n], refs[n:2 * n]
    o_ref, acc_ref, carry_ref, w_ref = refs[2 * n:]
    g = pl.program_id(1)

    @pl.when(g == 0)
    def _():
        acc_ref[...] = jnp.zeros(acc_ref.shape, F32)
        carry_ref[...] = jnp.zeros(carry_ref.shape, F32)
        w_ref[...] = jnp.zeros(w_ref.shape, F32)

    for h in range(SB_HEADS):
        acc = acc_ref[h]
        for p in range(n):
            acc = acc + w_ref[pl.ds(p * SB_HEADS + h, 1), :] * v_refs[p][0, h]
        acc_ref[h] = acc

    ws, carry = _decode_weights(q_ref[0], bias_ref[...], negu_ref[...],
                                [k_refs[p][0] for p in range(n)], carry_ref[...])
    for p in range(n):
        w_ref[p * SB_HEADS:(p + 1) * SB_HEADS, :] = ws[p]
    carry_ref[...] = carry

    @pl.when(g == pl.num_programs(1) - 1)
    def _():
        o_ref[0] = jnp.sum(acc_ref[...], axis=2, keepdims=True)


def _sb_decode(page_table, q, b_sb, kt_pages, vt_pages):
    nseq, n_pages = page_table.shape
    n = DECODE_PAGES
    bias = jnp.tile(b_sb.reshape(SB_HEADS, 1) * LOG2E, (n, 1))

    n_groups = n_pages // n

    def page_spec(p, lag):
        def index(b, g, pt):
            group = jnp.clip(g - lag, 0, n_groups - 1)
            return (pt[b, n_pages - 1 - (group * n + p)], 0, 0, 0)
        return pl.BlockSpec((1, SB_HEADS, SB_HEAD_DIM, PAGE_SIZE), index)

    q_spec = pl.BlockSpec((1, SB_HEADS, SB_HEAD_DIM, 1), lambda b, g, pt: (b, 0, 0, 0))
    grid_spec = pltpu.PrefetchScalarGridSpec(
        num_scalar_prefetch=1,
        grid=(nseq, n_groups + 1),
        in_specs=[q_spec,
                  pl.BlockSpec((n * SB_HEADS, 1), lambda b, g, pt: (0, 0)),
                  pl.BlockSpec((PAGE_SIZE, PAGE_SIZE), lambda b, g, pt: (0, 0))]
                 + [page_spec(p, 0) for p in range(n)] + [page_spec(p, 1) for p in range(n)],
        out_specs=q_spec,
        scratch_shapes=[pltpu.VMEM((SB_HEADS, SB_HEAD_DIM, PAGE_SIZE), F32),
                        pltpu.VMEM((SB_HEADS, 1), F32),
                        pltpu.VMEM((n * SB_HEADS, PAGE_SIZE), F32)],
    )
    return pl.pallas_call(
        _sb_decode_kernel,
        grid_spec=grid_spec,
        out_shape=jax.ShapeDtypeStruct((nseq, SB_HEADS, SB_HEAD_DIM, 1), F32),
        compiler_params=_params("arbitrary", "arbitrary"),
        name="sb_decode",
    )(page_table, q.reshape(nseq, SB_HEADS, SB_HEAD_DIM, 1), bias, _neg_upper(PAGE_SIZE),
      *([kt_pages] * n), *([vt_pages] * n))


def kernel(x_prompt, x_sample, mem_prompt, cache_k_pages, cache_v_pages, page_table, cache_mem_k,
           cache_mem_v, state_conv, g_mix_pre, w_in, b_sb, w_conv, g_mem, w_mem_kv, w_gate, b_gate,
           w_sb_o, w_conv_o, w_xa_o, w_o, g_mix_post, g_ffn_pre, w_up, w_down, g_ffn_post):
    depth = w_in.shape[0]
    assert depth == 1, "single-layer step"
    nb, seq, _ = x_prompt.shape
    nseq = x_sample.shape[0]
    bf = lambda a: a[0].astype(BF16)
    w_in_b, w_gate_b, w_mem_b = bf(w_in), bf(w_gate), bf(w_mem_kv)
    w_sb_b, w_cv_b, w_xa_b, w_o_b = bf(w_sb_o), bf(w_conv_o), bf(w_xa_o), bf(w_o)
    w_up_b, w_down_b = bf(w_up), bf(w_down)
    tail = (g_mix_pre, w_gate_b, b_gate, w_sb_b, w_cv_b, w_xa_b, w_o_b, g_mix_post)
    ffn_w = (g_ffn_pre, w_up_b, w_down_b, g_ffn_post)

    xs = x_sample.reshape(nseq, D_MODEL)
    q_s, k_s, v_s, yconv_s, c_s, xq_s = _proj_sample(
        xs, g_mix_pre, w_in_b, w_conv[0], state_conv[0, :, 0, :], state_conv[0, :, 1, :])
    kt_pages = jnp.transpose(cache_k_pages[0], (0, 2, 3, 1))
    vt_pages = jnp.transpose(cache_v_pages[0], (0, 2, 3, 1))

    kv, kv_b = _memkv(mem_prompt.reshape(nb * N_MEM, D_MODEL), g_mem, w_mem_b)
    mk = kv[:, :XA_WIDTH].reshape(1, nb, N_MEM, XA_HEADS, XA_HEAD_DIM)
    mv = kv[:, XA_WIDTH:].reshape(1, nb, N_MEM, XA_HEADS, XA_HEAD_DIM)
    mk_b = kv_b[:, :XA_WIDTH].reshape(nb, N_MEM, XA_WIDTH)
    mv_b = kv_b[:, XA_WIDTH:].reshape(nb, N_MEM, XA_WIDTH)
    q_p, k_p, v_p, kt, vt, yconv_p, cstate_p, yxa_p = _proj_prompt(
        x_prompt, g_mix_pre, w_in_b, w_conv[0], mk_b, mv_b, _bias_columns(b_sb[0]))
    ysb_p = _sb_prompt(q_p, k_p, v_p)
    h_p = _merge(x_prompt, ysb_p, yconv_p, yxa_p, *tail, tm=PROMPT_TILE)
    y_p, ysb_hosted = _ffn_decode(h_p, *ffn_w, PROMPT_TILE, page_table, q_s, b_sb[0],
                                  kt_pages, vt_pages)
    n_hosted = ysb_hosted.shape[0]
    to_cache = lambda a: jnp.transpose(
        a.reshape(nb, SB_HEADS, SB_HEAD_DIM, seq), (0, 3, 1, 2))[None]
    k_prompt, v_prompt = to_cache(kt), to_cache(vt)

    ysb_rest = _sb_decode(page_table[n_hosted:], q_s[n_hosted:], b_sb[0], kt_pages, vt_pages)
    ysb_s = jnp.concatenate([ysb_hosted, ysb_rest], axis=0)
    ysb_s = ysb_s.reshape(1, nseq, SB_WIDTH).astype(BF16)
    yxa_s = _xattn_sample(xq_s, cache_mem_k[0].reshape(nseq, N_MEM, XA_WIDTH),
                          cache_mem_v[0].reshape(nseq, N_MEM, XA_WIDTH))
    h_s = _merge(xs[None], ysb_s, yconv_s[None], yxa_s.reshape(1, nseq, XA_WIDTH), *tail, tm=nseq)
    y_s = _ffn(h_s, *ffn_w, tm=nseq)
    conv_sample = jnp.stack([state_conv[0, :, 1, :], c_s], axis=1)[None]
    to_tok = lambda a: a.reshape(1, nseq, 1, SB_HEADS, SB_HEAD_DIM)

    return (y_p, y_s.reshape(nseq, 1, D_MODEL), k_prompt, v_prompt, cstate_p[None], mk, mv,
            to_tok(k_s), to_tok(v_s), conv_sample)
```

```python
import functools
import math

import jax
import jax.numpy as jnp
from jax import lax
from jax.experimental import pallas as pl
from jax.experimental.pallas import tpu as pltpu

F32 = jnp.float32
BF16 = jnp.bfloat16

D_MODEL = 1024
SB_HEADS = 8
SB_HEAD_DIM = 64
SB_WIDTH = SB_HEADS * SB_HEAD_DIM
CONV_WIDTH = 512
CONV_K = 3
XA_HEADS = 4
XA_HEAD_DIM = 128
XA_WIDTH = XA_HEADS * XA_HEAD_DIM
N_MEM = 256
D_FF = 4 * D_MODEL
N_BRANCH = 3
RMS_EPS = 1e-6
PAGE_SIZE = 128
IN_CHUNK = 512
LOG2E = math.log2(math.e)
SB_QSCALE = LOG2E / math.sqrt(SB_HEAD_DIM)
BIAS_TERMS = 3
MASKED_LOG2 =-1e30
XA_SCALE = 1.0 / math.sqrt(XA_HEAD_DIM)

V7X_VMEM_BYTES = 64 * 1024 * 1024
VMEM_LIMIT = V7X_VMEM_BYTES - 8 * 1024 * 1024

PROMPT_TILE = 512
SB_TILE = 256
SB_HEAD_GROUP = 8
DECODE_PAGES = 16
FF_CHUNK = 512
HOST_CHUNKS = D_FF // FF_CHUNK
FFN_DECODE_PAGES = 8
MERGE_DECODE_PAGES = 4
HOSTED_INPUTS = 4
DECODE_RING = 4


def _params(*sem):
    return pltpu.CompilerParams(dimension_semantics=sem, vmem_limit_bytes=VMEM_LIMIT)


def _resident(shape):
    nd = len(shape)
    return pl.BlockSpec(shape, lambda *_: (0,) * nd, pipeline_mode=pl.Buffered(1))


def _rms(x, g):
    return x * lax.rsqrt(jnp.mean(x * x, axis=-1, keepdims=True) + RMS_EPS) * g


def _dot(a, b):
    return jnp.dot(a, b, preferred_element_type=F32)


def _dot_nt(a, b):
    return lax.dot_general(a, b, (((1,), (1,)), ((), ())), preferred_element_type=F32)


def _softplus2(z2):
    return jnp.maximum(z2, jnp.log2(1.0 + jnp.exp2(jnp.minimum(z2, 64.0))))


def _bias_columns(b_sb):
    rest = b_sb.astype(F32) * LOG2E
    cols = []
    for _ in range(BIAS_TERMS):
        piece = rest.astype(BF16).astype(F32)
        cols.append(piece)
        rest = rest - piece
    pad = jnp.zeros((b_sb.shape[0], SB_HEAD_DIM - BIAS_TERMS), F32)
    return jnp.concatenate([jnp.stack(cols, axis=1), pad], axis=1)


def _neg_upper(n):
    s = lax.broadcasted_iota(jnp.int32, (n, n), 0)
    j = lax.broadcasted_iota(jnp.int32, (n, n), 1)
    return jnp.where(s > j, -1.0, 0.0).astype(BF16)


def _memkv_kernel(mem_ref, g_ref, w_ref, kv_ref, kvb_ref):
    mn = _rms(mem_ref[...], g_ref[...]).astype(BF16)
    kv = _dot(mn, w_ref[...])
    kv_ref[...] = kv
    kvb_ref[...] = kv.astype(BF16)


def _memkv(mem2d, g_mem, w_mem_kv_b):
    rows = mem2d.shape[0]
    return pl.pallas_call(
        _memkv_kernel,
        out_shape=(jax.ShapeDtypeStruct((rows, 2 * XA_WIDTH), F32),
                   jax.ShapeDtypeStruct((rows, 2 * XA_WIDTH), BF16)),
        compiler_params=pltpu.CompilerParams(vmem_limit_bytes=VMEM_LIMIT),
        name="memkv",
    )(mem2d, g_mem, w_mem_kv_b)


def _proj_prompt_kernel(x_ref, g_ref, win_ref, wconv_ref, mk_ref, mv_ref, kpad_ref,
                        q_ref, k_ref, v_ref, kt_ref, vt_ref, yconv_ref, cstate_ref, yxa_ref,
                        hist_ref):
    tm = x_ref.shape[1]
    i = pl.program_id(1)
    xn = _rms(x_ref[0], g_ref[...]).astype(BF16)

    def proj(c):
        return _dot(xn, win_ref[:, c * IN_CHUNK:(c + 1) * IN_CHUNK])

    q = proj(0) * SB_QSCALE
    k = proj(1)
    v = proj(2)
    lane = lax.broadcasted_iota(jnp.int32, (tm, SB_HEAD_DIM), 1)
    qpad = jnp.where(lane < BIAS_TERMS, 1.0, 0.0)
    for h in range(SB_HEADS):
        sl = slice(h * SB_HEAD_DIM, (h + 1) * SB_HEAD_DIM)
        kpad = jnp.broadcast_to(kpad_ref[h:h + 1, :], (tm, SB_HEAD_DIM))
        q_ref[0, h] = jnp.concatenate([q[:, sl], qpad], axis=1).astype(BF16)
        k_ref[0, h] = jnp.concatenate([k[:, sl], kpad], axis=1).astype(BF16)
        v_ref[0, h] = v[:, sl].astype(BF16)
    kt_ref[0] = k.T
    vt_ref[0] = v.T

    cb = proj(3)
    c = proj(4) * proj(5)

    @pl.when(i == 0)
    def _():
        hist_ref[0:8, :] = jnp.zeros((8, CONV_WIDTH), F32)

    hist_ref[8:8 + tm, :] = c
    c_m1 = hist_ref[pl.ds(7, tm), :]
    c_m2 = hist_ref[pl.ds(6, tm), :]
    w = wconv_ref[...]
    yconv_ref[0] = (cb * (c_m2 * w[0:1] + c_m1 * w[1:2] + c * w[2:3])).astype(BF16)
    tail = c[tm - (CONV_K - 1):tm, :]
    hist_ref[6:8, :] = tail
    cstate_ref[0] = tail

    xq = proj(6) * XA_SCALE
    for h in range(XA_HEADS):
        sl = slice(h * XA_HEAD_DIM, (h + 1) * XA_HEAD_DIM)
        s = _dot_nt(xq[:, sl].astype(BF16), mk_ref[0, :, sl])
        e = jnp.exp(s - jnp.max(s, axis=-1, keepdims=True))
        o = _dot(e.astype(BF16), mv_ref[0, :, sl]) / jnp.sum(e, axis=-1, keepdims=True)
        yxa_ref[0, :, sl] = o.astype(BF16)


def _proj_prompt(x, g, w_in_b, w_conv, mk_b, mv_b, kpad):
    nb, seq, _ = x.shape
    tm = PROMPT_TILE
    head_spec = pl.BlockSpec((1, SB_HEADS, tm, SB_HEAD_DIM), lambda b, i: (b, 0, i, 0))
    wide_spec = pl.BlockSpec((1, SB_HEADS, tm, 2 * SB_HEAD_DIM), lambda b, i: (b, 0, i, 0))
    t_spec = pl.BlockSpec((1, SB_WIDTH, tm), lambda b, i: (b, 0, i))
    row_spec = lambda w: pl.BlockSpec((1, tm, w), lambda b, i: (b, i, 0))
    mem_spec = pl.BlockSpec((1, N_MEM, XA_WIDTH), lambda b, i: (b, 0, 0))
    head_shape = jax.ShapeDtypeStruct((nb, SB_HEADS, seq, SB_HEAD_DIM), BF16)
    wide_shape = jax.ShapeDtypeStruct((nb, SB_HEADS, seq, 2 * SB_HEAD_DIM), BF16)
    t_shape = jax.ShapeDtypeStruct((nb, SB_WIDTH, seq), F32)
    return pl.pallas_call(
        _proj_prompt_kernel,
        grid=(nb, seq // tm),
        in_specs=[row_spec(D_MODEL), _resident(g.shape), _resident(w_in_b.shape),
                  _resident(w_conv.shape), mem_spec, mem_spec, _resident(kpad.shape)],
        out_specs=(wide_spec, wide_spec, head_spec, t_spec, t_spec, row_spec(CONV_WIDTH),
                   pl.BlockSpec((1, CONV_K - 1, CONV_WIDTH), lambda b, i: (b, 0, 0)),
                   row_spec(XA_WIDTH)),
        out_shape=(wide_shape, wide_shape, head_shape, t_shape, t_shape,
                   jax.ShapeDtypeStruct((nb, seq, CONV_WIDTH), BF16),
                   jax.ShapeDtypeStruct((nb, CONV_K - 1, CONV_WIDTH), F32),
                   jax.ShapeDtypeStruct((nb, seq, XA_WIDTH), BF16)),
        scratch_shapes=[pltpu.VMEM((tm + 8, CONV_WIDTH), F32)],
        compiler_params=_params("arbitrary", "arbitrary"),
        name="proj_prompt",
    )(x, g, w_in_b, w_conv, mk_b, mv_b, kpad)


def _sb_prompt_kernel(q_ref, k_ref, v_ref, negu_ref, o_ref, z_scr, p_scr, zmp_scr, w_scr, acc_scr,
                      carry_scr):
    ng, t = q_ref.shape[1], q_ref.shape[2]
    i = pl.program_id(2)
    negu = negu_ref[...]

    def rows(j):
        return pl.ds(pl.multiple_of(j * t, t), t)

    def logits(g, j):
        return _dot_nt(q_ref[0, g], k_ref[0, g, rows(j), :])

    def keep_and_beta(z):
        p = _softplus2(z)
        return p, z - p

    row = lax.broadcasted_iota(jnp.int32, (t, t), 0)
    col = lax.broadcasted_iota(jnp.int32, (t, t), 1)
    causal = col < row
    for g in range(ng):
        p, zmp = keep_and_beta(logits(g, i))
        p_scr[g] = jnp.where(causal, p, 0.0).astype(BF16)
        zmp_scr[g] = jnp.where(causal, zmp, MASKED_LOG2)
        z_scr[g] = logits(g, jnp.maximum(i - 1, 0))
        w_scr[g] = jnp.zeros((t, t), BF16)
        acc_scr[g] = jnp.zeros((t, SB_HEAD_DIM), F32)
        carry_scr[g] = jnp.zeros((t, 1), F32)

    def trip(n, _):
        j_prev = jnp.minimum(i - n + 1, i)
        j_next2 = jnp.maximum(i - n - 2, 0)
        for g in range(ng):
            acc_scr[g] += _dot(w_scr[g], v_ref[0, g, rows(j_prev), :])
            p_cur = p_scr[g]
            after = _dot(p_cur, negu)
            keep = after + carry_scr[g]
            w = jnp.exp2(zmp_scr[g] + keep)
            carry_scr[g] = keep[:, 0:1] - p_cur[:, 0:1].astype(F32)
            p, zmp = keep_and_beta(z_scr[g])
            z_scr[g] = logits(g, j_next2)
            w_scr[g] = w.astype(BF16)
            p_scr[g] = p.astype(BF16)
            zmp_scr[g] = zmp
        return 0

    lax.fori_loop(0, i + 1, trip, 0)
    heads = [acc_scr[g] + _dot(w_scr[g], v_ref[0, g, 0:t, :]) for g in range(ng)]
    o_ref[0] = jnp.concatenate(heads, axis=1).astype(BF16)


def _sb_prompt(q, k, v):
    nb, nh, seq, dh = v.shape
    t, ng = SB_TILE, SB_HEAD_GROUP
    kv_spec = lambda w: pl.BlockSpec((1, ng, seq, w), lambda b, h, i: (b, h, 0, 0),
                                     pipeline_mode=pl.Buffered(1))
    qo_spec = lambda w: pl.BlockSpec((1, ng, t, w), lambda b, h, i: (b, h, i, 0))
    return pl.pallas_call(
        _sb_prompt_kernel,
        grid=(nb, nh // ng, seq // t),
        in_specs=[qo_spec(q.shape[3]), kv_spec(k.shape[3]), kv_spec(dh), _resident((t, t))],
        out_specs=pl.BlockSpec((1, t, ng * dh), lambda b, h, i: (b, i, h)),
        out_shape=jax.ShapeDtypeStruct((nb, seq, nh * dh), BF16),
        scratch_shapes=[pltpu.VMEM((ng, t, t), F32),
                        pltpu.VMEM((ng, t, t), BF16), pltpu.VMEM((ng, t, t), F32),
                        pltpu.VMEM((ng, t, t), BF16), pltpu.VMEM((ng, t, dh), F32),
                        pltpu.VMEM((ng, t, 1), F32)],
        compiler_params=_params("arbitrary", "arbitrary", "arbitrary"),
        name="sb_prompt",
    )(q, k, v, _neg_upper(t))


def _merge_kernel(x_ref, ysb_ref, yconv_ref, yxa_ref, gpre_ref, wgate_ref, bgate_ref,
                  wsb_ref, wcv_ref, wxa_ref, wo_ref, gpost_ref, h_ref):
    _merge_tile(x_ref, ysb_ref, yconv_ref, yxa_ref, gpre_ref, wgate_ref, bgate_ref,
                wsb_ref, wcv_ref, wxa_ref, wo_ref, gpost_ref, h_ref, _NoSideWork())


def _merge_tile(x_ref, ysb_ref, yconv_ref, yxa_ref, gpre_ref, wgate_ref, bgate_ref,
                wsb_ref, wcv_ref, wxa_ref, wo_ref, gpost_ref, h_ref, side):
    x = x_ref[0]
    xn = _rms(x, gpre_ref[...]).astype(BF16)
    half = D_MODEL // 2

    def gate(br):
        sl = slice(br * D_MODEL, (br + 1) * D_MODEL)
        return 1.0 / (1.0 + jnp.exp(-(_dot(xn, wgate_ref[:, sl]) + bgate_ref[:, sl])))

    branches = ((ysb_ref, wsb_ref), (yconv_ref, wcv_ref), (yxa_ref, wxa_ref))
    m = None
    for br, (y_ref, w_ref) in enumerate(branches):
        side.chunk(2 * br)
        g = gate(br)
        side.advance(2 * br)
        side.chunk(2 * br + 1)
        y = g * _dot(y_ref[0], w_ref[...])
        m = y if m is None else m + y
        side.advance(2 * br + 1)
    mb = m.astype(BF16)
    side.chunk(6)
    mo_lo = _dot(mb, wo_ref[:, :half])
    side.advance(6)
    side.chunk(7)
    mo_hi = _dot(mb, wo_ref[:, half:])
    side.advance(7)
    h_ref[0] = x + _rms(jnp.concatenate([mo_lo, mo_hi], axis=1), gpost_ref[...])


def _merge(x, ysb, yconv, yxa, g_pre, w_gate_b, b_gate, w_sb_b, w_cv_b, w_xa_b, w_o_b, g_post, tm):
    nb, seq, _ = x.shape
    row_spec = lambda w: pl.BlockSpec((1, tm, w), lambda b, i: (b, i, 0))
    consts = (g_pre, w_gate_b, b_gate, w_sb_b, w_cv_b, w_xa_b, w_o_b, g_post)
    return pl.pallas_call(
        _merge_kernel,
        grid=(nb, seq // tm),
        in_specs=[row_spec(D_MODEL), row_spec(SB_WIDTH), row_spec(CONV_WIDTH), row_spec(XA_WIDTH)]
                 + [_resident(c.shape) for c in consts],
        out_specs=row_spec(D_MODEL),
        out_shape=jax.ShapeDtypeStruct((nb, seq, D_MODEL), F32),
        compiler_params=_params("arbitrary", "arbitrary"),
        name="merge",
    )(x, ysb, yconv, yxa, *consts)


def _merge_decode_kernel(pt_ref, x_ref, ysb_ref, yconv_ref, yxa_ref, gpre_ref, wgate_ref, bgate_ref,
                         wsb_ref, wcv_ref, wxa_ref, wo_ref, gpost_ref, *rest):
    h_ref = rest[HOSTED_INPUTS]
    side = _HostedDecode(pt_ref, rest[:HOSTED_INPUTS], rest[HOSTED_INPUTS + 1:], MERGE_DECODE_PAGES)
    side.begin()
    _merge_tile(x_ref, ysb_ref, yconv_ref, yxa_ref, gpre_ref, wgate_ref, bgate_ref,
                wsb_ref, wcv_ref, wxa_ref, wo_ref, gpost_ref, h_ref, side)
    side.finish()


def _merge_decode(x, ysb, yconv, yxa, g_pre, w_gate_b, b_gate, w_sb_b, w_cv_b, w_xa_b, w_o_b, g_post,
                  tm, decode_args, first_seq):
    nb, seq, _ = x.shape
    row_spec = lambda w: pl.BlockSpec((1, tm, w), lambda b, i, pt: (b, i, 0))
    consts = (g_pre, w_gate_b, b_gate, w_sb_b, w_cv_b, w_xa_b, w_o_b, g_post)
    host = _hosted_decode_setup(decode_args, MERGE_DECODE_PAGES, nb, seq // tm, first_seq)
    grid_spec = pltpu.PrefetchScalarGridSpec(
        num_scalar_prefetch=1,
        grid=(nb, seq // tm),
        in_specs=[row_spec(D_MODEL), row_spec(SB_WIDTH), row_spec(CONV_WIDTH), row_spec(XA_WIDTH)]
                 + [_resident(c.shape) for c in consts] + host.in_specs,
        out_specs=(row_spec(D_MODEL), host.out_spec),
        scratch_shapes=host.scratch_shapes,
    )
    return pl.pallas_call(
        _merge_decode_kernel,
        grid_spec=grid_spec,
        out_shape=(jax.ShapeDtypeStruct((nb, seq, D_MODEL), F32), host.out_shape),
        compiler_params=_params("arbitrary", "arbitrary"),
        name="merge_decode",
    )(host.page_table, x, ysb, yconv, yxa, *consts, *host.operands)


def _ffn_tile(h_ref, gpre_ref, wup_ref, wdown_ref, gpost_ref, o_ref, side):
    h = h_ref[0]
    hn = _rms(h, gpre_ref[...]).astype(BF16)
    f = jnp.zeros(h.shape, F32)
    for c in range(HOST_CHUNKS):
        side.chunk(c)
        sl = slice(c * FF_CHUNK, (c + 1) * FF_CHUNK)
        a = jnp.maximum(_dot(hn, wup_ref[:, sl]), 0.0)
        f = f + _dot((a * a).astype(BF16), wdown_ref[sl, :])
        side.advance(c)
    o_ref[0] = h + _rms(f, gpost_ref[...])


def _ffn_kernel(h_ref, gpre_ref, wup_ref, wdown_ref, gpost_ref, o_ref):
    _ffn_tile(h_ref, gpre_ref, wup_ref, wdown_ref, gpost_ref, o_ref, _NoSideWork())


def _ffn(h, g_pre, w_up_b, w_down_b, g_post, tm):
    nb, seq, _ = h.shape
    row_spec = pl.BlockSpec((1, tm, D_MODEL), lambda b, i: (b, i, 0))
    consts = (g_pre, w_up_b, w_down_b, g_post)
    return pl.pallas_call(
        _ffn_kernel,
        grid=(nb, seq // tm),
        in_specs=[row_spec] + [_resident(c.shape) for c in consts],
        out_specs=row_spec,
        out_shape=jax.ShapeDtypeStruct((nb, seq, D_MODEL), F32),
        compiler_params=_params("arbitrary", "arbitrary"),
        name="ffn",
    )(h, *consts)


class _NoSideWork:
    def chunk(self, c):
        pass

    def advance(self, c):
        pass


class _HostedDecode:
    def __init__(self, pt_ref, in_refs, out_and_scratch, n):
        self.pt = pt_ref
        self.q, self.bias, self.kt, self.vt = in_refs
        self.ysb, self.kbuf, self.vbuf, self.sems, self.acc, self.carry = out_and_scratch
        self.n = n
        self.n_pages = pt_ref.shape[1]
        self.steps_per_seq = self.n_pages // (HOST_CHUNKS * n)
        self.step = pl.program_id(0) * pl.num_programs(1) + pl.program_id(1)
        self.n_steps = pl.num_programs(0) * pl.num_programs(1)
        self.part = self.step % self.steps_per_seq

    def _copies(self, steps_ahead, c):
        step = self.step + steps_ahead
        seq = step // self.steps_per_seq
        first = ((step % self.steps_per_seq) * HOST_CHUNKS + c) * self.n
        slot = c % DECODE_RING
        out = []
        for p in range(self.n):
            page = self.pt[seq, self.n_pages - 1 - (first + p)]
            out.append(pltpu.make_async_copy(self.kt.at[page], self.kbuf.at[slot, p],
                                             self.sems.at[0, slot, p]))
            out.append(pltpu.make_async_copy(self.vt.at[page], self.vbuf.at[slot, p],
                                             self.sems.at[1, slot, p]))
        return out

    def begin(self):
        @pl.when(self.step == 0)
        def _():
            for c in range(DECODE_RING):
                for cp in self._copies(0, c):
                    cp.start()

        @pl.when(self.part == 0)
        def _():
            self.acc[...] = jnp.zeros(self.acc.shape, F32)
            self.carry[...] = jnp.zeros(self.carry.shape, F32)

        for cp in self._copies(0, 0):
            cp.wait()

    def chunk(self, c):
        slot = c % DECODE_RING
        ws, carry = _decode_weights(self.q[0], self.bias[...],
                                    [self.kbuf[slot, p] for p in range(self.n)], self.carry[...])
        self.carry[...] = carry
        for h in range(SB_HEADS):
            acc = self.acc[h]
            for p in range(self.n):
                acc = acc + ws[p][h:h + 1, :] * self.vbuf[slot, p, h]
            self.acc[h] = acc

    def advance(self, c):
        steps_ahead, c_ahead = divmod(c + DECODE_RING, HOST_CHUNKS)
        if steps_ahead == 0:
            for cp in self._copies(0, c_ahead):
                cp.start()
        else:
            @pl.when(self.step + steps_ahead < self.n_steps)
            def _():
                for cp in self._copies(steps_ahead, c_ahead):
                    cp.start()
        if c + 1 < HOST_CHUNKS:
            for cp in self._copies(0, c + 1):
                cp.wait()

    def finish(self):
        @pl.when(self.part == self.steps_per_seq - 1)
        def _():
            self.ysb[0] = jnp.sum(self.acc[...], axis=2, keepdims=True)


class _HostedDecodeSetup:
    pass


def _hosted_decode_setup(decode_args, n, nb, steps_per_b, first_seq):
    page_table, q, b_sb, kt_pages, vt_pages = decode_args
    n_pages = page_table.shape[1]
    pages_per_step = HOST_CHUNKS * n
    assert n_pages % pages_per_step == 0 and HOST_CHUNKS % DECODE_RING == 0
    steps_per_seq = n_pages // pages_per_step
    n_hosted = nb * steps_per_b // steps_per_seq
    assert first_seq + n_hosted <= page_table.shape[0]
    page_shape = (SB_HEADS, SB_HEAD_DIM, PAGE_SIZE)
    bias = jnp.tile(b_sb.reshape(SB_HEADS, 1) * LOG2E, (n, 1))
    seq_spec = pl.BlockSpec((1, SB_HEADS, SB_HEAD_DIM, 1),
                            lambda b, i, pt: ((b * steps_per_b + i) // steps_per_seq, 0, 0, 0))
    host = _HostedDecodeSetup()
    host.n_hosted = n_hosted
    host.page_table = page_table[first_seq:first_seq + n_hosted]
    host.operands = (q[first_seq:first_seq + n_hosted].reshape(n_hosted, SB_HEADS, SB_HEAD_DIM, 1),
                     bias, kt_pages, vt_pages)
    host.in_specs = [seq_spec, _resident(bias.shape),
                     pl.BlockSpec(memory_space=pl.ANY), pl.BlockSpec(memory_space=pl.ANY)]
    host.out_spec = seq_spec
    host.out_shape = jax.ShapeDtypeStruct((n_hosted, SB_HEADS, SB_HEAD_DIM, 1), F32)
    host.scratch_shapes = [pltpu.VMEM((DECODE_RING, n) + page_shape, F32),
                           pltpu.VMEM((DECODE_RING, n) + page_shape, F32),
                           pltpu.SemaphoreType.DMA((2, DECODE_RING, n)),
                           pltpu.VMEM(page_shape, F32),
                           pltpu.VMEM((SB_HEADS, 1), F32)]
    return host


def _ffn_decode_kernel(pt_ref, h_ref, gpre_ref, wup_ref, wdown_ref, gpost_ref, *rest):
    o_ref = rest[HOSTED_INPUTS]
    side = _HostedDecode(pt_ref, rest[:HOSTED_INPUTS], rest[HOSTED_INPUTS + 1:], FFN_DECODE_PAGES)
    side.begin()
    _ffn_tile(h_ref, gpre_ref, wup_ref, wdown_ref, gpost_ref, o_ref, side)
    side.finish()


def _ffn_decode(h, g_pre, w_up_b, w_down_b, g_post, tm, decode_args, first_seq):
    nb, seq, _ = h.shape
    row_spec = pl.BlockSpec((1, tm, D_MODEL), lambda b, i, pt: (b, i, 0))
    consts = (g_pre, w_up_b, w_down_b, g_post)
    host = _hosted_decode_setup(decode_args, FFN_DECODE_PAGES, nb, seq // tm, first_seq)
    grid_spec = pltpu.PrefetchScalarGridSpec(
        num_scalar_prefetch=1,
        grid=(nb, seq // tm),
        in_specs=[row_spec] + [_resident(c.shape) for c in consts] + host.in_specs,
        out_specs=(row_spec, host.out_spec),
        scratch_shapes=host.scratch_shapes,
    )
    return pl.pallas_call(
        _ffn_decode_kernel,
        grid_spec=grid_spec,
        out_shape=(jax.ShapeDtypeStruct((nb, seq, D_MODEL), F32), host.out_shape),
        compiler_params=_params("arbitrary", "arbitrary"),
        name="ffn_decode",
    )(host.page_table, h, *consts, *host.operands)


def _proj_sample_kernel(x_ref, g_ref, win_ref, wconv_ref, s0_ref, s1_ref,
                        q_ref, k_ref, v_ref, yconv_ref, c_ref, xq_ref):
    xn = _rms(x_ref[...], g_ref[...]).astype(BF16)

    def proj(c):
        return _dot(xn, win_ref[:, c * IN_CHUNK:(c + 1) * IN_CHUNK])

    q_ref[...] = proj(0) * SB_QSCALE
    k_ref[...] = proj(1)
    v_ref[...] = proj(2)
    cb = proj(3)
    c = proj(4) * proj(5)
    w = wconv_ref[...]
    yconv_ref[...] = (cb * (s0_ref[...] * w[0:1] + s1_ref[...] * w[1:2] + c * w[2:3])).astype(BF16)
    c_ref[...] = c
    xq_ref[...] = proj(6) * XA_SCALE


def _proj_sample(x2d, g, w_in_b, w_conv, s0, s1):
    n = x2d.shape[0]
    wide = jax.ShapeDtypeStruct((n, IN_CHUNK), F32)
    return pl.pallas_call(
        _proj_sample_kernel,
        out_shape=(wide, wide, wide, jax.ShapeDtypeStruct((n, CONV_WIDTH), BF16), wide, wide),
        compiler_params=pltpu.CompilerParams(vmem_limit_bytes=VMEM_LIMIT),
        name="proj_sample",
    )(x2d, g, w_in_b, w_conv, s0, s1)


def _xattn_sample_kernel(xq_ref, mk_ref, mv_ref, o_ref):
    xq = xq_ref[0]
    for h in range(XA_HEADS):
        sl = slice(h * XA_HEAD_DIM, (h + 1) * XA_HEAD_DIM)
        qh = jnp.broadcast_to(xq[:, sl], (8, XA_HEAD_DIM)).astype(BF16)
        s = _dot_nt(qh, mk_ref[0, :, sl].astype(BF16))
        e = jnp.exp(s - jnp.max(s, axis=-1, keepdims=True))
        o = _dot(e.astype(BF16), mv_ref[0, :, sl].astype(BF16)) / jnp.sum(e, axis=-1, keepdims=True)
        o_ref[0, :, sl] = o[0:1].astype(BF16)


def _xattn_sample(xq, mk, mv):
    n = xq.shape[0]
    q_spec = pl.BlockSpec((1, 1, XA_WIDTH), lambda b: (b, 0, 0))
    m_spec = pl.BlockSpec((1, N_MEM, XA_WIDTH), lambda b: (b, 0, 0))
    return pl.pallas_call(
        _xattn_sample_kernel,
        grid=(n,),
        in_specs=[q_spec, m_spec, m_spec],
        out_specs=q_spec,
        out_shape=jax.ShapeDtypeStruct((n, 1, XA_WIDTH), BF16),
        compiler_params=_params("arbitrary"),
        name="xattn_sample",
    )(xq.reshape(n, 1, XA_WIDTH), mk, mv)


def _decode_weights(qb, bias, k_pages, carry):
    z = jnp.concatenate([jnp.sum(qb * kp, axis=1) for kp in k_pages], axis=0) + bias
    p_all = _softplus2(z)
    lane = lax.broadcasted_iota(jnp.int32, p_all.shape, 1)
    suffix = p_all
    shift = 1
    while shift < PAGE_SIZE:
        moved = pltpu.roll(suffix, PAGE_SIZE - shift, axis=1)
        suffix = suffix + jnp.where(lane < PAGE_SIZE - shift, moved, 0.0)
        shift *= 2
    ws = []
    for p in range(len(k_pages)):
        sl = slice(p * SB_HEADS, (p + 1) * SB_HEADS)
        ws.append(jnp.exp2((z[sl] - suffix[sl]) + carry))
        carry = carry - suffix[sl][:, 0:1]
    return ws, carry


def _sb_decode_kernel(pt_ref, q_ref, bias_ref, *refs):
    del pt_ref
    n = DECODE_PAGES
    k_refs, v_refs = refs[:n], refs[n:2 * n]
    o_ref, acc_ref, carry_ref, w_ref = refs[2 * n:]
    g = pl.program_id(1)

    @pl.when(g == 0)
    def _():
        acc_ref[...] = jnp.zeros(acc_ref.shape, F32)
        carry_ref[...] = jnp.zeros(carry_ref.shape, F32)
        w_ref[...] = jnp.zeros(w_ref.shape, F32)

    for h in range(SB_HEADS):
        acc = acc_ref[h]
        for p in range(n):
            acc = acc + w_ref[pl.ds(p * SB_HEADS + h, 1), :] * v_refs[p][0, h]
        acc_ref[h] = acc

    ws, carry = _decode_weights(q_ref[0], bias_ref[...],
                                [k_refs[p][0] for p in range(n)], carry_ref[...])
    for p in range(n):
        w_ref[p * SB_HEADS:(p + 1) * SB_HEADS, :] = ws[p]
    carry_ref[...] = carry

    @pl.when(g == pl.num_programs(1) - 1)
    def _():
        o_ref[0] = jnp.sum(acc_ref[...], axis=2, keepdims=True)


def _sb_decode(page_table, q, b_sb, kt_pages, vt_pages):
    nseq, n_pages = page_table.shape
    n = DECODE_PAGES
    bias = jnp.tile(b_sb.reshape(SB_HEADS, 1) * LOG2E, (n, 1))

    n_groups = n_pages // n

    def page_spec(p, lag):
        def index(b, g, pt):
            group = jnp.clip(g - lag, 0, n_groups - 1)
            return (pt[b, n_pages - 1 - (group * n + p)], 0, 0, 0)
        return pl.BlockSpec((1, SB_HEADS, SB_HEAD_DIM, PAGE_SIZE), index)

    q_spec = pl.BlockSpec((1, SB_HEADS, SB_HEAD_DIM, 1), lambda b, g, pt: (b, 0, 0, 0))
    grid_spec = pltpu.PrefetchScalarGridSpec(
        num_scalar_prefetch=1,
        grid=(nseq, n_groups + 1),
        in_specs=[q_spec,
                  pl.BlockSpec((n * SB_HEADS, 1), lambda b, g, pt: (0, 0))]
                 + [page_spec(p, 0) for p in range(n)] + [page_spec(p, 1) for p in range(n)],
        out_specs=q_spec,
        scratch_shapes=[pltpu.VMEM((SB_HEADS, SB_HEAD_DIM, PAGE_SIZE), F32),
                        pltpu.VMEM((SB_HEADS, 1), F32),
                        pltpu.VMEM((n * SB_HEADS, PAGE_SIZE), F32)],
    )
    return pl.pallas_call(
        _sb_decode_kernel,
        grid_spec=grid_spec,
        out_shape=jax.ShapeDtypeStruct((nseq, SB_HEADS, SB_HEAD_DIM, 1), F32),
        compiler_params=_params("arbitrary", "arbitrary"),
        name="sb_decode",
    )(page_table, q.reshape(nseq, SB_HEADS, SB_HEAD_DIM, 1), bias,
      *([kt_pages] * n), *([vt_pages] * n))


def kernel(x_prompt, x_sample, mem_prompt, cache_k_pages, cache_v_pages, page_table, cache_mem_k,
           cache_mem_v, state_conv, g_mix_pre, w_in, b_sb, w_conv, g_mem, w_mem_kv, w_gate, b_gate,
           w_sb_o, w_conv_o, w_xa_o, w_o, g_mix_post, g_ffn_pre, w_up, w_down, g_ffn_post):
    depth = w_in.shape[0]
    assert depth == 1, "single-layer step"
    nb, seq, _ = x_prompt.shape
    nseq = x_sample.shape[0]
    bf = lambda a: a[0].astype(BF16)
    w_in_b, w_gate_b, w_mem_b = bf(w_in), bf(w_gate), bf(w_mem_kv)
    w_sb_b, w_cv_b, w_xa_b, w_o_b = bf(w_sb_o), bf(w_conv_o), bf(w_xa_o), bf(w_o)
    w_up_b, w_down_b = bf(w_up), bf(w_down)
    tail = (g_mix_pre, w_gate_b, b_gate, w_sb_b, w_cv_b, w_xa_b, w_o_b, g_mix_post)
    ffn_w = (g_ffn_pre, w_up_b, w_down_b, g_ffn_post)

    xs = x_sample.reshape(nseq, D_MODEL)
    q_s, k_s, v_s, yconv_s, c_s, xq_s = _proj_sample(
        xs, g_mix_pre, w_in_b, w_conv[0], state_conv[0, :, 0, :], state_conv[0, :, 1, :])
    kt_pages = jnp.transpose(cache_k_pages[0], (0, 2, 3, 1))
    vt_pages = jnp.transpose(cache_v_pages[0], (0, 2, 3, 1))

    kv, kv_b = _memkv(mem_prompt.reshape(nb * N_MEM, D_MODEL), g_mem, w_mem_b)
    mk = kv[:, :XA_WIDTH].reshape(1, nb, N_MEM, XA_HEADS, XA_HEAD_DIM)
    mv = kv[:, XA_WIDTH:].reshape(1, nb, N_MEM, XA_HEADS, XA_HEAD_DIM)
    mk_b = kv_b[:, :XA_WIDTH].reshape(nb, N_MEM, XA_WIDTH)
    mv_b = kv_b[:, XA_WIDTH:].reshape(nb, N_MEM, XA_WIDTH)
    q_p, k_p, v_p, kt, vt, yconv_p, cstate_p, yxa_p = _proj_prompt(
        x_prompt, g_mix_pre, w_in_b, w_conv[0], mk_b, mv_b, _bias_columns(b_sb[0]))
    ysb_p = _sb_prompt(q_p, k_p, v_p)
    decode_args = (page_table, q_s, b_sb[0], kt_pages, vt_pages)
    h_p, ysb_a = _merge_decode(x_prompt, ysb_p, yconv_p, yxa_p, *tail, PROMPT_TILE, decode_args, 0)
    y_p, ysb_b = _ffn_decode(h_p, *ffn_w, PROMPT_TILE, decode_args, ysb_a.shape[0])
    n_hosted = ysb_a.shape[0] + ysb_b.shape[0]
    to_cache = lambda a: jnp.transpose(
        a.reshape(nb, SB_HEADS, SB_HEAD_DIM, seq), (0, 3, 1, 2))[None]
    k_prompt, v_prompt = to_cache(kt), to_cache(vt)

    ysb_rest = _sb_decode(page_table[n_hosted:], q_s[n_hosted:], b_sb[0], kt_pages, vt_pages)
    ysb_s = jnp.concatenate([ysb_a, ysb_b, ysb_rest], axis=0)
    ysb_s = ysb_s.reshape(1, nseq, SB_WIDTH).astype(BF16)
    yxa_s = _xattn_sample(xq_s, cache_mem_k[0].reshape(nseq, N_MEM, XA_WIDTH),
                          cache_mem_v[0].reshape(nseq, N_MEM, XA_WIDTH))
    h_s = _merge(xs[None], ysb_s, yconv_s[None], yxa_s.reshape(1, nseq, XA_WIDTH), *tail, tm=nseq)
    y_s = _ffn(h_s, *ffn_w, tm=nseq)
    conv_sample = jnp.stack([state_conv[0, :, 1, :], c_s], axis=1)[None]
    to_tok = lambda a: a.reshape(1, nseq, 1, SB_HEADS, SB_HEAD_DIM)

    return (y_p, y_s.reshape(nseq, 1, D_MODEL), k_prompt, v_prompt, cstate_p[None], mk, mv,
            to_tok(k_s), to_tok(v_s), conv_sample)
```

```python
import math

import jax
import jax.numpy as jnp
from jax import lax
from jax.experimental import pallas as pl
from jax.experimental.pallas import tpu as pltpu

F32 = jnp.float32
BF16 = jnp.bfloat16

D_MODEL = 1024
SB_HEADS = 8
SB_HEAD_DIM = 64
SB_WIDTH = SB_HEADS * SB_HEAD_DIM
CONV_WIDTH = 512
CONV_K = 3
XA_HEADS = 4
XA_HEAD_DIM = 128
XA_WIDTH = XA_HEADS * XA_HEAD_DIM
N_MEM = 256
D_FF = 4 * D_MODEL
RMS_EPS = 1e-6
PAGE_SIZE = 128
IN_CHUNK = 512
LOG2E = math.log2(math.e)
SB_QSCALE = LOG2E / math.sqrt(SB_HEAD_DIM)
BIAS_TERMS = 3
MASKED_LOG2 = -1e30
XA_SCALE = 1.0 / math.sqrt(XA_HEAD_DIM)

V7X_VMEM_BYTES = 64 * 1024 * 1024
VMEM_LIMIT = V7X_VMEM_BYTES - 8 * 1024 * 1024

PROMPT_TILE = 512
SB_TILE = 256
SB_HEAD_GROUP = 8
FF_CHUNK = 256
HOST_CHUNKS = D_FF // FF_CHUNK
FFN_DECODE_PAGES = 8
XA_SAMPLE_BLOCK = 4
HOSTED_INPUTS = 4
DECODE_RING = 4


def _params(*sem):
    return pltpu.CompilerParams(dimension_semantics=sem, vmem_limit_bytes=VMEM_LIMIT)


def _resident(shape):
    nd = len(shape)
    return pl.BlockSpec(shape, lambda *_: (0,) * nd, pipeline_mode=pl.Buffered(1))


def _rms(x, g):
    return x * lax.rsqrt(jnp.mean(x * x, axis=-1, keepdims=True) + RMS_EPS) * g


def _dot(a, b):
    return jnp.dot(a, b, preferred_element_type=F32)


def _dot_nt(a, b):
    return lax.dot_general(a, b, (((1,), (1,)), ((), ())), preferred_element_type=F32)


def _softplus2(z2):
    return jnp.maximum(z2, jnp.log2(1.0 + jnp.exp2(jnp.minimum(z2, 64.0))))


def _bias_columns(b_sb):
    rest = b_sb.astype(F32) * LOG2E
    cols = []
    for _ in range(BIAS_TERMS):
        piece = rest.astype(BF16).astype(F32)
        cols.append(piece)
        rest = rest - piece
    pad = jnp.zeros((b_sb.shape[0], SB_HEAD_DIM - BIAS_TERMS), F32)
    return jnp.concatenate([jnp.stack(cols, axis=1), pad], axis=1)


def _neg_upper(n):
    s = lax.broadcasted_iota(jnp.int32, (n, n), 0)
    j = lax.broadcasted_iota(jnp.int32, (n, n), 1)
    return jnp.where(s > j, -1.0, 0.0).astype(BF16)


def _memkv_kernel(mem_ref, g_ref, w_ref, kv_ref, kvb_ref):
    mn = _rms(mem_ref[...], g_ref[...]).astype(BF16)
    kv = _dot(mn, w_ref[...])
    kv_ref[...] = kv
    kvb_ref[...] = kv.astype(BF16)


def _memkv(mem2d, g_mem, w_mem_kv_b):
    rows = mem2d.shape[0]
    return pl.pallas_call(
        _memkv_kernel,
        out_shape=(jax.ShapeDtypeStruct((rows, 2 * XA_WIDTH), F32),
                   jax.ShapeDtypeStruct((rows, 2 * XA_WIDTH), BF16)),
        compiler_params=pltpu.CompilerParams(vmem_limit_bytes=VMEM_LIMIT),
        name="memkv",
    )(mem2d, g_mem, w_mem_kv_b)


def _proj_prompt_kernel(x_ref, g_ref, win_ref, wconv_ref, mk_ref, mv_ref, kpad_ref,
                        q_ref, k_ref, v_ref, kt_ref, vt_ref, yconv_ref, cstate_ref, yxa_ref,
                        hist_ref):
    tm = x_ref.shape[1]
    i = pl.program_id(1)
    xn = _rms(x_ref[0], g_ref[...]).astype(BF16)

    def proj(c):
        return _dot(xn, win_ref[:, c * IN_CHUNK:(c + 1) * IN_CHUNK])

    q = proj(0) * SB_QSCALE
    k = proj(1)
    v = proj(2)
    lane = lax.broadcasted_iota(jnp.int32, (tm, SB_HEAD_DIM), 1)
    qpad = jnp.where(lane < BIAS_TERMS, 1.0, 0.0)
    for h in range(SB_HEADS):
        sl = slice(h * SB_HEAD_DIM, (h + 1) * SB_HEAD_DIM)
        kpad = jnp.broadcast_to(kpad_ref[h:h + 1, :], (tm, SB_HEAD_DIM))
        q_ref[0, h] = jnp.concatenate([q[:, sl], qpad], axis=1).astype(BF16)
        k_ref[0, h] = jnp.concatenate([k[:, sl], kpad], axis=1).astype(BF16)
        v_ref[0, h] = v[:, sl].astype(BF16)
    kt_ref[0] = k.T
    vt_ref[0] = v.T

    cb = proj(3)
    c = proj(4) * proj(5)

    @pl.when(i == 0)
    def _():
        hist_ref[0:8, :] = jnp.zeros((8, CONV_WIDTH), F32)

    hist_ref[8:8 + tm, :] = c
    c_m1 = hist_ref[pl.ds(7, tm), :]
    c_m2 = hist_ref[pl.ds(6, tm), :]
    w = wconv_ref[...]
    yconv_ref[0] = (cb * (c_m2 * w[0:1] + c_m1 * w[1:2] + c * w[2:3])).astype(BF16)
    tail = c[tm - (CONV_K - 1):tm, :]
    hist_ref[6:8, :] = tail
    cstate_ref[0] = tail

    xq = proj(6) * XA_SCALE
    for h in range(XA_HEADS):
        sl = slice(h * XA_HEAD_DIM, (h + 1) * XA_HEAD_DIM)
        s = _dot_nt(xq[:, sl].astype(BF16), mk_ref[0, :, sl])
        e = jnp.exp(s - jnp.max(s, axis=-1, keepdims=True))
        o = _dot(e.astype(BF16), mv_ref[0, :, sl]) / jnp.sum(e, axis=-1, keepdims=True)
        yxa_ref[0, :, sl] = o.astype(BF16)


def _proj_prompt(x, g, w_in_b, w_conv, mk_b, mv_b, kpad):
    nb, seq, _ = x.shape
    tm = PROMPT_TILE
    head_spec = pl.BlockSpec((1, SB_HEADS, tm, SB_HEAD_DIM), lambda b, i: (b, 0, i, 0))
    wide_spec = pl.BlockSpec((1, SB_HEADS, tm, 2 * SB_HEAD_DIM), lambda b, i: (b, 0, i, 0))
    t_spec = pl.BlockSpec((1, SB_WIDTH, tm), lambda b, i: (b, 0, i))
    row_spec = lambda w: pl.BlockSpec((1, tm, w), lambda b, i: (b, i, 0))
    mem_spec = pl.BlockSpec((1, N_MEM, XA_WIDTH), lambda b, i: (b, 0, 0))
    head_shape = jax.ShapeDtypeStruct((nb, SB_HEADS, seq, SB_HEAD_DIM), BF16)
    wide_shape = jax.ShapeDtypeStruct((nb, SB_HEADS, seq, 2 * SB_HEAD_DIM), BF16)
    t_shape = jax.ShapeDtypeStruct((nb, SB_WIDTH, seq), F32)
    return pl.pallas_call(
        _proj_prompt_kernel,
        grid=(nb, seq // tm),
        in_specs=[row_spec(D_MODEL), _resident(g.shape), _resident(w_in_b.shape),
                  _resident(w_conv.shape), mem_spec, mem_spec, _resident(kpad.shape)],
        out_specs=(wide_spec, wide_spec, head_spec, t_spec, t_spec, row_spec(CONV_WIDTH),
                   pl.BlockSpec((1, CONV_K - 1, CONV_WIDTH), lambda b, i: (b, 0, 0)),
                   row_spec(XA_WIDTH)),
        out_shape=(wide_shape, wide_shape, head_shape, t_shape, t_shape,
                   jax.ShapeDtypeStruct((nb, seq, CONV_WIDTH), BF16),
                   jax.ShapeDtypeStruct((nb, CONV_K - 1, CONV_WIDTH), F32),
                   jax.ShapeDtypeStruct((nb, seq, XA_WIDTH), BF16)),
        scratch_shapes=[pltpu.VMEM((tm + 8, CONV_WIDTH), F32)],
        compiler_params=_params("arbitrary", "arbitrary"),
        name="proj_prompt",
    )(x, g, w_in_b, w_conv, mk_b, mv_b, kpad)


def _sb_prompt_kernel(q_ref, k_ref, v_ref, negu_ref, o_ref, z_scr, p_scr, zmp_scr, w_scr, acc_scr,
                      carry_scr):
    ng, t = q_ref.shape[1], q_ref.shape[2]
    i = pl.program_id(2)
    negu = negu_ref[...]

    def rows(j):
        return pl.ds(pl.multiple_of(j * t, t), t)

    def logits(g, j):
        return _dot_nt(q_ref[0, g], k_ref[0, g, rows(j), :])

    def keep_and_beta(z):
        p = _softplus2(z)
        return p, z - p

    row = lax.broadcasted_iota(jnp.int32, (t, t), 0)
    col = lax.broadcasted_iota(jnp.int32, (t, t), 1)
    causal = col < row
    for g in range(ng):
        p, zmp = keep_and_beta(logits(g, i))
        p_scr[g] = jnp.where(causal, p, 0.0).astype(BF16)
        zmp_scr[g] = jnp.where(causal, zmp, MASKED_LOG2)
        z_scr[g] = logits(g, jnp.maximum(i - 1, 0))
        w_scr[g] = jnp.zeros((t, t), BF16)
        acc_scr[g] = jnp.zeros((t, SB_HEAD_DIM), F32)
        carry_scr[g] = jnp.zeros((t, 1), F32)

    def trip(n, _):
        j_prev = jnp.minimum(i - n + 1, i)
        j_next2 = jnp.maximum(i - n - 2, 0)
        for g in range(ng):
            acc_scr[g] += _dot(w_scr[g], v_ref[0, g, rows(j_prev), :])
            p_cur = p_scr[g]
            after = _dot(p_cur, negu)
            keep = after + carry_scr[g]
            w = jnp.exp2(zmp_scr[g] + keep)
            carry_scr[g] = keep[:, 0:1] - p_cur[:, 0:1].astype(F32)
            p, zmp = keep_and_beta(z_scr[g])
            z_scr[g] = logits(g, j_next2)
            w_scr[g] = w.astype(BF16)
            p_scr[g] = p.astype(BF16)
            zmp_scr[g] = zmp
        return 0

    lax.fori_loop(0, i + 1, trip, 0)
    heads = [acc_scr[g] + _dot(w_scr[g], v_ref[0, g, 0:t, :]) for g in range(ng)]
    o_ref[0] = jnp.concatenate(heads, axis=1).astype(BF16)


def _sb_prompt(q, k, v):
    nb, nh, seq, dh = v.shape
    t, ng = SB_TILE, SB_HEAD_GROUP
    kv_spec = lambda w: pl.BlockSpec((1, ng, seq, w), lambda b, h, i: (b, h, 0, 0),
                                     pipeline_mode=pl.Buffered(1))
    qo_spec = lambda w: pl.BlockSpec((1, ng, t, w), lambda b, h, i: (b, h, i, 0))
    return pl.pallas_call(
        _sb_prompt_kernel,
        grid=(nb, nh // ng, seq // t),
        in_specs=[qo_spec(q.shape[3]), kv_spec(k.shape[3]), kv_spec(dh), _resident((t, t))],
        out_specs=pl.BlockSpec((1, t, ng * dh), lambda b, h, i: (b, i, h)),
        out_shape=jax.ShapeDtypeStruct((nb, seq, nh * dh), BF16),
        scratch_shapes=[pltpu.VMEM((ng, t, t), F32),
                        pltpu.VMEM((ng, t, t), BF16), pltpu.VMEM((ng, t, t), F32),
                        pltpu.VMEM((ng, t, t), BF16), pltpu.VMEM((ng, t, dh), F32),
                        pltpu.VMEM((ng, t, 1), F32)],
        compiler_params=_params("arbitrary", "arbitrary", "arbitrary"),
        name="sb_prompt",
    )(q, k, v, _neg_upper(t))


def _merge_kernel(x_ref, ysb_ref, yconv_ref, yxa_ref, gpre_ref, wgate_ref, bgate_ref,
                  wsb_ref, wcv_ref, wxa_ref, wo_ref, gpost_ref, h_ref):
    x = x_ref[0]
    xn = _rms(x, gpre_ref[...]).astype(BF16)

    def gate(br):
        sl = slice(br * D_MODEL, (br + 1) * D_MODEL)
        return 1.0 / (1.0 + jnp.exp(-(_dot(xn, wgate_ref[:, sl]) + bgate_ref[:, sl])))

    m = gate(0) * _dot(ysb_ref[0], wsb_ref[...])
    m = m + gate(1) * _dot(yconv_ref[0], wcv_ref[...])
    m = m + gate(2) * _dot(yxa_ref[0], wxa_ref[...])
    mo = _dot(m.astype(BF16), wo_ref[...])
    h_ref[0] = x + _rms(mo, gpost_ref[...])


def _merge(x, ysb, yconv, yxa, g_pre, w_gate_b, b_gate, w_sb_b, w_cv_b, w_xa_b, w_o_b, g_post, tm):
    nb, seq, _ = x.shape
    row_spec = lambda w: pl.BlockSpec((1, tm, w), lambda b, i: (b, i, 0))
    consts = (g_pre, w_gate_b, b_gate, w_sb_b, w_cv_b, w_xa_b, w_o_b, g_post)
    return pl.pallas_call(
        _merge_kernel,
        grid=(nb, seq // tm),
        in_specs=[row_spec(D_MODEL), row_spec(SB_WIDTH), row_spec(CONV_WIDTH), row_spec(XA_WIDTH)]
                 + [_resident(c.shape) for c in consts],
        out_specs=row_spec(D_MODEL),
        out_shape=jax.ShapeDtypeStruct((nb, seq, D_MODEL), F32),
        compiler_params=_params("arbitrary", "arbitrary"),
        name="merge",
    )(x, ysb, yconv, yxa, *consts)


def _ffn_tile(h_ref, gpre_ref, wup_ref, wdown_ref, gpost_ref, o_ref, side):
    h = h_ref[0]
    hn = _rms(h, gpre_ref[...]).astype(BF16)
    f = jnp.zeros(h.shape, F32)
    for c in range(HOST_CHUNKS):
        side.chunk(c)
        sl = slice(c * FF_CHUNK, (c + 1) * FF_CHUNK)
        a = jnp.maximum(_dot(hn, wup_ref[:, sl]), 0.0)
        f = f + _dot((a * a).astype(BF16), wdown_ref[sl, :])
        side.advance(c)
    o_ref[0] = h + _rms(f, gpost_ref[...])


def _ffn_kernel(h_ref, gpre_ref, wup_ref, wdown_ref, gpost_ref, o_ref):
    _ffn_tile(h_ref, gpre_ref, wup_ref, wdown_ref, gpost_ref, o_ref, _NoSideWork())


def _ffn(h, g_pre, w_up_b, w_down_b, g_post, tm):
    nb, seq, _ = h.shape
    row_spec = pl.BlockSpec((1, tm, D_MODEL), lambda b, i: (b, i, 0))
    consts = (g_pre, w_up_b, w_down_b, g_post)
    return pl.pallas_call(
        _ffn_kernel,
        grid=(nb, seq // tm),
        in_specs=[row_spec] + [_resident(c.shape) for c in consts],
        out_specs=row_spec,
        out_shape=jax.ShapeDtypeStruct((nb, seq, D_MODEL), F32),
        compiler_params=_params("arbitrary", "arbitrary"),
        name="ffn",
    )(h, *consts)


class _NoSideWork:
    def chunk(self, c):
        pass

    def advance(self, c):
        pass


class _HostedDecode:
    def __init__(self, pt_ref, in_refs, out_and_scratch, n):
        self.pt = pt_ref
        self.q, self.bias, self.kt, self.vt = in_refs
        self.ysb, self.kbuf, self.vbuf, self.sems, self.acc, self.carry = out_and_scratch
        self.n = n
        self.n_pages = pt_ref.shape[1]
        self.steps_per_seq = self.n_pages // (HOST_CHUNKS * n)
        self.step = pl.program_id(0) * pl.num_programs(1) + pl.program_id(1)
        self.n_steps = pl.num_programs(0) * pl.num_programs(1)
        self.part = self.step % self.steps_per_seq

    def _copies(self, steps_ahead, c):
        step = self.step + steps_ahead
        seq = step // self.steps_per_seq
        first = ((step % self.steps_per_seq) * HOST_CHUNKS + c) * self.n
        slot = c % DECODE_RING
        out = []
        for p in range(self.n):
            page = self.pt[seq, self.n_pages - 1 - (first + p)]
            out.append(pltpu.make_async_copy(self.kt.at[page], self.kbuf.at[slot, p],
                                             self.sems.at[0, slot, p]))
            out.append(pltpu.make_async_copy(self.vt.at[page], self.vbuf.at[slot, p],
                                             self.sems.at[1, slot, p]))
        return out

    def begin(self):
        @pl.when(self.step == 0)
        def _():
            for c in range(DECODE_RING):
                for cp in self._copies(0, c):
                    cp.start()

        @pl.when(self.part == 0)
        def _():
            self.acc[...] = jnp.zeros(self.acc.shape, F32)
            self.carry[...] = jnp.zeros(self.carry.shape, F32)

        for cp in self._copies(0, 0):
            cp.wait()

    def chunk(self, c):
        slot = c % DECODE_RING
        ws, carry = _decode_weights(self.q[0], self.bias[...],
                                    [self.kbuf[slot, p] for p in range(self.n)], self.carry[...])
        self.carry[...] = carry
        for h in range(SB_HEADS):
            acc = self.acc[h]
            for p in range(self.n):
                acc = acc + ws[p][h:h + 1, :] * self.vbuf[slot, p, h]
            self.acc[h] = acc

    def advance(self, c):
        steps_ahead, c_ahead = divmod(c + DECODE_RING, HOST_CHUNKS)
        if steps_ahead == 0:
            for cp in self._copies(0, c_ahead):
                cp.start()
        else:
            @pl.when(self.step + steps_ahead < self.n_steps)
            def _():
                for cp in self._copies(steps_ahead, c_ahead):
                    cp.start()
        if c + 1 < HOST_CHUNKS:
            for cp in self._copies(0, c + 1):
                cp.wait()

    def finish(self):
        @pl.when(self.part == self.steps_per_seq - 1)
        def _():
            self.ysb[0] = jnp.sum(self.acc[...], axis=2, keepdims=True)


class _HostedDecodeSetup:
    pass


def _hosted_decode_setup(decode_args, n, nb, steps_per_b):
    page_table, q, b_sb, kt_pages, vt_pages = decode_args
    n_pages = page_table.shape[1]
    pages_per_step = HOST_CHUNKS * n
    assert n_pages % pages_per_step == 0 and HOST_CHUNKS % DECODE_RING == 0
    steps_per_seq = n_pages // pages_per_step
    n_hosted = page_table.shape[0]
    assert n_hosted * steps_per_seq == nb * steps_per_b, "host grid must cover every decode sequence"
    page_shape = (SB_HEADS, SB_HEAD_DIM, PAGE_SIZE)
    bias = jnp.tile(b_sb.reshape(SB_HEADS, 1) * LOG2E, (n, 1))
    seq_spec = pl.BlockSpec((1, SB_HEADS, SB_HEAD_DIM, 1),
                            lambda b, i, pt: ((b * steps_per_b + i) // steps_per_seq, 0, 0, 0))
    host = _HostedDecodeSetup()
    host.page_table = page_table
    host.operands = (q.reshape(n_hosted, SB_HEADS, SB_HEAD_DIM, 1), bias, kt_pages, vt_pages)
    host.in_specs = [seq_spec, _resident(bias.shape),
                     pl.BlockSpec(memory_space=pl.ANY), pl.BlockSpec(memory_space=pl.ANY)]
    host.out_spec = seq_spec
    host.out_shape = jax.ShapeDtypeStruct((n_hosted, SB_HEADS, SB_HEAD_DIM, 1), F32)
    host.scratch_shapes = [pltpu.VMEM((DECODE_RING, n) + page_shape, F32),
                           pltpu.VMEM((DECODE_RING, n) + page_shape, F32),
                           pltpu.SemaphoreType.DMA((2, DECODE_RING, n)),
                           pltpu.VMEM(page_shape, F32),
                           pltpu.VMEM((SB_HEADS, 1), F32)]
    return host


def _ffn_decode_kernel(pt_ref, h_ref, gpre_ref, wup_ref, wdown_ref, gpost_ref, *rest):
    o_ref = rest[HOSTED_INPUTS]
    side = _HostedDecode(pt_ref, rest[:HOSTED_INPUTS], rest[HOSTED_INPUTS + 1:], FFN_DECODE_PAGES)
    side.begin()
    _ffn_tile(h_ref, gpre_ref, wup_ref, wdown_ref, gpost_ref, o_ref, side)
    side.finish()


def _ffn_decode(h, g_pre, w_up_b, w_down_b, g_post, tm, decode_args):
    nb, seq, _ = h.shape
    row_spec = pl.BlockSpec((1, tm, D_MODEL), lambda b, i, pt: (b, i, 0))
    consts = (g_pre, w_up_b, w_down_b, g_post)
    host = _hosted_decode_setup(decode_args, FFN_DECODE_PAGES, nb, seq // tm)
    grid_spec = pltpu.PrefetchScalarGridSpec(
        num_scalar_prefetch=1,
        grid=(nb, seq // tm),
        in_specs=[row_spec] + [_resident(c.shape) for c in consts] + host.in_specs,
        out_specs=(row_spec, host.out_spec),
        scratch_shapes=host.scratch_shapes,
    )
    return pl.pallas_call(
        _ffn_decode_kernel,
        grid_spec=grid_spec,
        out_shape=(jax.ShapeDtypeStruct((nb, seq, D_MODEL), F32), host.out_shape),
        compiler_params=_params("arbitrary", "arbitrary"),
        name="ffn_decode",
    )(host.page_table, h, *consts, *host.operands)


def _proj_sample_kernel(x_ref, g_ref, win_ref, wconv_ref, s0_ref, s1_ref,
                        q_ref, k_ref, v_ref, yconv_ref, c_ref, xq_ref):
    xn = _rms(x_ref[...], g_ref[...]).astype(BF16)

    def proj(c):
        return _dot(xn, win_ref[:, c * IN_CHUNK:(c + 1) * IN_CHUNK])

    q_ref[...] = proj(0) * SB_QSCALE
    k_ref[...] = proj(1)
    v_ref[...] = proj(2)
    cb = proj(3)
    c = proj(4) * proj(5)
    w = wconv_ref[...]
    yconv_ref[...] = (cb * (s0_ref[...] * w[0:1] + s1_ref[...] * w[1:2] + c * w[2:3])).astype(BF16)
    c_ref[...] = c
    xq_ref[...] = proj(6) * XA_SCALE


def _proj_sample(x2d, g, w_in_b, w_conv, s0, s1):
    n = x2d.shape[0]
    wide = jax.ShapeDtypeStruct((n, IN_CHUNK), F32)
    return pl.pallas_call(
        _proj_sample_kernel,
        out_shape=(wide, wide, wide, jax.ShapeDtypeStruct((n, CONV_WIDTH), BF16), wide, wide),
        compiler_params=pltpu.CompilerParams(vmem_limit_bytes=VMEM_LIMIT),
        name="proj_sample",
    )(x2d, g, w_in_b, w_conv, s0, s1)


def _xattn_sample_kernel(xq_ref, mk_ref, mv_ref, o_ref):
    for s in range(xq_ref.shape[0]):
        xq = xq_ref[s]
        for h in range(XA_HEADS):
            sl = slice(h * XA_HEAD_DIM, (h + 1) * XA_HEAD_DIM)
            qh = jnp.broadcast_to(xq[:, sl], (8, XA_HEAD_DIM)).astype(BF16)
            sc = _dot_nt(qh, mk_ref[s, :, sl].astype(BF16))
            e = jnp.exp(sc - jnp.max(sc, axis=-1, keepdims=True))
            o = _dot(e.astype(BF16), mv_ref[s, :, sl].astype(BF16)) / jnp.sum(e, axis=-1, keepdims=True)
            o_ref[s, :, sl] = o[0:1].astype(BF16)


def _xattn_sample(xq, mk, mv):
    n = xq.shape[0]
    nblk = XA_SAMPLE_BLOCK
    q_spec = pl.BlockSpec((nblk, 1, XA_WIDTH), lambda b: (b, 0, 0))
    m_spec = pl.BlockSpec((nblk, N_MEM, XA_WIDTH), lambda b: (b, 0, 0))
    return pl.pallas_call(
        _xattn_sample_kernel,
        grid=(n // nblk,),
        in_specs=[q_spec, m_spec, m_spec],
        out_specs=q_spec,
        out_shape=jax.ShapeDtypeStruct((n, 1, XA_WIDTH), BF16),
        compiler_params=_params("arbitrary"),
        name="xattn_sample",
    )(xq.reshape(n, 1, XA_WIDTH), mk, mv)


def _decode_weights(qb, bias, k_pages, carry):
    z = jnp.concatenate([jnp.sum(qb * kp, axis=1) for kp in k_pages], axis=0) + bias
    p_all = _softplus2(z)
    lane = lax.broadcasted_iota(jnp.int32, p_all.shape, 1)
    suffix = p_all
    shift = 1
    while shift < PAGE_SIZE:
        moved = pltpu.roll(suffix, PAGE_SIZE - shift, axis=1)
        suffix = suffix + jnp.where(lane < PAGE_SIZE - shift, moved, 0.0)
        shift *= 2
    ws = []
    for p in range(len(k_pages)):
        sl = slice(p * SB_HEADS, (p + 1) * SB_HEADS)
        ws.append(jnp.exp2((z[sl] - suffix[sl]) + carry))
        carry = carry - suffix[sl][:, 0:1]
    return ws, carry


def kernel(x_prompt, x_sample, mem_prompt, cache_k_pages, cache_v_pages, page_table, cache_mem_k,
           cache_mem_v, state_conv, g_mix_pre, w_in, b_sb, w_conv, g_mem, w_mem_kv, w_gate, b_gate,
           w_sb_o, w_conv_o, w_xa_o, w_o, g_mix_post, g_ffn_pre, w_up, w_down, g_ffn_post):
    depth = w_in.shape[0]
    assert depth == 1, "single-layer step"
    nb, seq, _ = x_prompt.shape
    nseq = x_sample.shape[0]
    bf = lambda a: a[0].astype(BF16)
    w_in_b, w_gate_b, w_mem_b = bf(w_in), bf(w_gate), bf(w_mem_kv)
    w_sb_b, w_cv_b, w_xa_b, w_o_b = bf(w_sb_o), bf(w_conv_o), bf(w_xa_o), bf(w_o)
    w_up_b, w_down_b = bf(w_up), bf(w_down)
    tail = (g_mix_pre, w_gate_b, b_gate, w_sb_b, w_cv_b, w_xa_b, w_o_b, g_mix_post)
    ffn_w = (g_ffn_pre, w_up_b, w_down_b, g_ffn_post)

    xs = x_sample.reshape(nseq, D_MODEL)
    q_s, k_s, v_s, yconv_s, c_s, xq_s = _proj_sample(
        xs, g_mix_pre, w_in_b, w_conv[0], state_conv[0, :, 0, :], state_conv[0, :, 1, :])
    kt_pages = jnp.transpose(cache_k_pages[0], (0, 2, 3, 1))
    vt_pages = jnp.transpose(cache_v_pages[0], (0, 2, 3, 1))

    kv, kv_b = _memkv(mem_prompt.reshape(nb * N_MEM, D_MODEL), g_mem, w_mem_b)
    mk = kv[:, :XA_WIDTH].reshape(1, nb, N_MEM, XA_HEADS, XA_HEAD_DIM)
    mv = kv[:, XA_WIDTH:].reshape(1, nb, N_MEM, XA_HEADS, XA_HEAD_DIM)
    mk_b = kv_b[:, :XA_WIDTH].reshape(nb, N_MEM, XA_WIDTH)
    mv_b = kv_b[:, XA_WIDTH:].reshape(nb, N_MEM, XA_WIDTH)
    q_p, k_p, v_p, kt, vt, yconv_p, cstate_p, yxa_p = _proj_prompt(
        x_prompt, g_mix_pre, w_in_b, w_conv[0], mk_b, mv_b, _bias_columns(b_sb[0]))
    ysb_p = _sb_prompt(q_p, k_p, v_p)
    h_p = _merge(x_prompt, ysb_p, yconv_p, yxa_p, *tail, tm=PROMPT_TILE)
    y_p, ysb_s = _ffn_decode(h_p, *ffn_w, PROMPT_TILE, (page_table, q_s, b_sb[0], kt_pages, vt_pages))
    to_cache = lambda a: jnp.transpose(
        a.reshape(nb, SB_HEADS, SB_HEAD_DIM, seq), (0, 3, 1, 2))[None]
    k_prompt, v_prompt = to_cache(kt), to_cache(vt)

    ysb_s = ysb_s.reshape(1, nseq, SB_WIDTH).astype(BF16)
    yxa_s = _xattn_sample(xq_s, cache_mem_k[0].reshape(nseq, N_MEM, XA_WIDTH),
                          cache_mem_v[0].reshape(nseq, N_MEM, XA_WIDTH))
    h_s = _merge(xs[None], ysb_s, yconv_s[None], yxa_s.reshape(1, nseq, XA_WIDTH), *tail, tm=nseq)
    y_s = _ffn(h_s, *ffn_w, tm=nseq)
    conv_sample = jnp.stack([state_conv[0, :, 1, :], c_s], axis=1)[None]
    to_tok = lambda a: a.reshape(1, nseq, 1, SB_HEADS, SB_HEAD_DIM)

    return (y_p, y_s.reshape(nseq, 1, D_MODEL), k_prompt, v_prompt, cstate_p[None], mk, mv,
            to_tok(k_s), to_tok(v_s), conv_sample)
```

```python
import math

import jax
import jax.numpy as jnp
from jax import lax
from jax.experimental import pallas as pl
from jax.experimental.pallas import tpu as pltpu

F32 = jnp.float32
BF16 = jnp.bfloat16

D_MODEL = 1024
SB_HEADS = 8
SB_HEAD_DIM = 64
SB_WIDTH = SB_HEADS * SB_HEAD_DIM
CONV_WIDTH = 512
CONV_K = 3
XA_HEADS = 4
XA_HEAD_DIM = 128
XA_WIDTH = XA_HEADS * XA_HEAD_DIM
N_MEM = 256
D_FF = 4 * D_MODEL
RMS_EPS = 1e-6
PAGE_SIZE = 128
IN_CHUNK = 512
LOG2E = math.log2(math.e)
SB_QSCALE = LOG2E / math.sqrt(SB_HEAD_DIM)
BIAS_TERMS = 3
MASKED_LOG2 = -1e30
XA_SCALE = 1.0 / math.sqrt(XA_HEAD_DIM)

V7X_VMEM_BYTES = 64 * 1024 * 1024
VMEM_LIMIT = V7X_VMEM_BYTES - 8 * 1024 * 1024

PROMPT_TILE = 512
SB_TILE = 256
SB_HEAD_GROUP = 8
FF_CHUNK = 256
HOST_CHUNKS = D_FF // FF_CHUNK
FFN_DECODE_PAGES = 8
XA_SAMPLE_BLOCK = 8
HOSTED_INPUTS = 4
DECODE_RING = 4


def _params(*sem):
    return pltpu.CompilerParams(dimension_semantics=sem, vmem_limit_bytes=VMEM_LIMIT)


def _resident(shape):
    nd = len(shape)
    return pl.BlockSpec(shape, lambda *_: (0,) * nd, pipeline_mode=pl.Buffered(1))


def _rms(x, g):
    return x * lax.rsqrt(jnp.mean(x * x, axis=-1, keepdims=True) + RMS_EPS) * g


def _dot(a, b):
    return jnp.dot(a, b, preferred_element_type=F32)


def _dot_nt(a, b):
    return lax.dot_general(a, b, (((1,), (1,)), ((), ())), preferred_element_type=F32)


def _softplus2(z2):
    return jnp.maximum(z2, jnp.log2(1.0 + jnp.exp2(jnp.minimum(z2, 64.0))))


def _bias_columns(b_sb):
    rest = b_sb.astype(F32) * LOG2E
    cols = []
    for _ in range(BIAS_TERMS):
        piece = rest.astype(BF16).astype(F32)
        cols.append(piece)
        rest = rest - piece
    pad = jnp.zeros((b_sb.shape[0], SB_HEAD_DIM - BIAS_TERMS), F32)
    return jnp.concatenate([jnp.stack(cols, axis=1), pad], axis=1)


def _neg_upper(n):
    s = lax.broadcasted_iota(jnp.int32, (n, n), 0)
    j = lax.broadcasted_iota(jnp.int32, (n, n), 1)
    return jnp.where(s > j, -1.0, 0.0).astype(BF16)


def _memkv_kernel(mem_ref, g_ref, w_ref, kv_ref, kvb_ref):
    mn = _rms(mem_ref[...], g_ref[...]).astype(BF16)
    kv = _dot(mn, w_ref[...])
    kv_ref[...] = kv
    kvb_ref[...] = kv.astype(BF16)


def _memkv(mem2d, g_mem, w_mem_kv_b):
    rows = mem2d.shape[0]
    return pl.pallas_call(
        _memkv_kernel,
        out_shape=(jax.ShapeDtypeStruct((rows, 2 * XA_WIDTH), F32),
                   jax.ShapeDtypeStruct((rows, 2 * XA_WIDTH), BF16)),
        compiler_params=pltpu.CompilerParams(vmem_limit_bytes=VMEM_LIMIT),
        name="memkv",
    )(mem2d, g_mem, w_mem_kv_b)


def _proj_prompt_kernel(x_ref, g_ref, win_ref, wconv_ref, mk_ref, mv_ref, kpad_ref,
                        q_ref, k_ref, v_ref, kt_ref, vt_ref, yconv_ref, cstate_ref, yxa_ref,
                        hist_ref):
    tm = x_ref.shape[1]
    i = pl.program_id(1)
    xn = _rms(x_ref[0], g_ref[...]).astype(BF16)

    def proj(c):
        return _dot(xn, win_ref[:, c * IN_CHUNK:(c + 1) * IN_CHUNK])

    q = proj(0) * SB_QSCALE
    k = proj(1)
    v = proj(2)
    lane = lax.broadcasted_iota(jnp.int32, (tm, SB_HEAD_DIM), 1)
    qpad = jnp.where(lane < BIAS_TERMS, 1.0, 0.0)
    for h in range(SB_HEADS):
        sl = slice(h * SB_HEAD_DIM, (h + 1) * SB_HEAD_DIM)
        kpad = jnp.broadcast_to(kpad_ref[h:h + 1, :], (tm, SB_HEAD_DIM))
        q_ref[0, h] = jnp.concatenate([q[:, sl], qpad], axis=1).astype(BF16)
        k_ref[0, h] = jnp.concatenate([k[:, sl], kpad], axis=1).astype(BF16)
        v_ref[0, h] = v[:, sl].astype(BF16)
    kt_ref[0] = k.T
    vt_ref[0] = v.T

    cb = proj(3)
    c = proj(4) * proj(5)

    @pl.when(i == 0)
    def _():
        hist_ref[0:8, :] = jnp.zeros((8, CONV_WIDTH), F32)

    hist_ref[8:8 + tm, :] = c
    c_m1 = hist_ref[pl.ds(7, tm), :]
    c_m2 = hist_ref[pl.ds(6, tm), :]
    w = wconv_ref[...]
    yconv_ref[0] = (cb * (c_m2 * w[0:1] + c_m1 * w[1:2] + c * w[2:3])).astype(BF16)
    tail = c[tm - (CONV_K - 1):tm, :]
    hist_ref[6:8, :] = tail
    cstate_ref[0] = tail

    xq = proj(6) * XA_SCALE
    for h in range(XA_HEADS):
        sl = slice(h * XA_HEAD_DIM, (h + 1) * XA_HEAD_DIM)
        s = _dot_nt(xq[:, sl].astype(BF16), mk_ref[0, :, sl])
        e = jnp.exp(s - jnp.max(s, axis=-1, keepdims=True))
        o = _dot(e.astype(BF16), mv_ref[0, :, sl]) / jnp.sum(e, axis=-1, keepdims=True)
        yxa_ref[0, :, sl] = o.astype(BF16)


def _proj_prompt(x, g, w_in_b, w_conv, mk_b, mv_b, kpad):
    nb, seq, _ = x.shape
    tm = PROMPT_TILE
    head_spec = pl.BlockSpec((1, SB_HEADS, tm, SB_HEAD_DIM), lambda b, i: (b, 0, i, 0))
    wide_spec = pl.BlockSpec((1, SB_HEADS, tm, 2 * SB_HEAD_DIM), lambda b, i: (b, 0, i, 0))
    t_spec = pl.BlockSpec((1, SB_WIDTH, tm), lambda b, i: (b, 0, i))
    row_spec = lambda w: pl.BlockSpec((1, tm, w), lambda b, i: (b, i, 0))
    mem_spec = pl.BlockSpec((1, N_MEM, XA_WIDTH), lambda b, i: (b, 0, 0))
    head_shape = jax.ShapeDtypeStruct((nb, SB_HEADS, seq, SB_HEAD_DIM), BF16)
    wide_shape = jax.ShapeDtypeStruct((nb, SB_HEADS, seq, 2 * SB_HEAD_DIM), BF16)
    t_shape = jax.ShapeDtypeStruct((nb, SB_WIDTH, seq), F32)
    return pl.pallas_call(
        _proj_prompt_kernel,
        grid=(nb, seq // tm),
        in_specs=[row_spec(D_MODEL), _resident(g.shape), _resident(w_in_b.shape),
                  _resident(w_conv.shape), mem_spec, mem_spec, _resident(kpad.shape)],
        out_specs=(wide_spec, wide_spec, head_spec, t_spec, t_spec, row_spec(CONV_WIDTH),
                   pl.BlockSpec((1, CONV_K - 1, CONV_WIDTH), lambda b, i: (b, 0, 0)),
                   row_spec(XA_WIDTH)),
        out_shape=(wide_shape, wide_shape, head_shape, t_shape, t_shape,
                   jax.ShapeDtypeStruct((nb, seq, CONV_WIDTH), BF16),
                   jax.ShapeDtypeStruct((nb, CONV_K - 1, CONV_WIDTH), F32),
                   jax.ShapeDtypeStruct((nb, seq, XA_WIDTH), BF16)),
        scratch_shapes=[pltpu.VMEM((tm + 8, CONV_WIDTH), F32)],
        compiler_params=_params("arbitrary", "arbitrary"),
        name="proj_prompt",
    )(x, g, w_in_b, w_conv, mk_b, mv_b, kpad)


def _sb_prompt_kernel(q_ref, k_ref, v_ref, negu_ref, o_ref, z_scr, p_scr, zmp_scr, w_scr, acc_scr,
                      carry_scr):
    ng, t = q_ref.shape[1], q_ref.shape[2]
    i = pl.program_id(2)
    negu = negu_ref[...]

    def rows(j):
        return pl.ds(pl.multiple_of(j * t, t), t)

    def logits(g, j):
        return _dot_nt(q_ref[0, g], k_ref[0, g, rows(j), :])

    def keep_and_beta(z):
        p = _softplus2(z)
        return p, z - p

    row = lax.broadcasted_iota(jnp.int32, (t, t), 0)
    col = lax.broadcasted_iota(jnp.int32, (t, t), 1)
    causal = col < row
    for g in range(ng):
        p, zmp = keep_and_beta(logits(g, i))
        p_scr[g] = jnp.where(causal, p, 0.0).astype(BF16)
        zmp_scr[g] = jnp.where(causal, zmp, MASKED_LOG2)
        z_scr[g] = logits(g, jnp.maximum(i - 1, 0))
        w_scr[g] = jnp.zeros((t, t), BF16)
        acc_scr[g] = jnp.zeros((t, SB_HEAD_DIM), F32)
        carry_scr[g] = jnp.zeros((t, 1), F32)

    def trip(n, _):
        j_prev = jnp.minimum(i - n + 1, i)
        j_next2 = jnp.maximum(i - n - 2, 0)
        for g in range(ng):
            acc_scr[g] += _dot(w_scr[g], v_ref[0, g, rows(j_prev), :])
            p_cur = p_scr[g]
            after = _dot(p_cur, negu)
            keep = after + carry_scr[g]
            w = jnp.exp2(zmp_scr[g] + keep)
            carry_scr[g] = keep[:, 0:1] - p_cur[:, 0:1].astype(F32)
            p, zmp = keep_and_beta(z_scr[g])
            z_scr[g] = logits(g, j_next2)
            w_scr[g] = w.astype(BF16)
            p_scr[g] = p.astype(BF16)
            zmp_scr[g] = zmp
        return 0

    lax.fori_loop(0, i + 1, trip, 0)
    heads = [acc_scr[g] + _dot(w_scr[g], v_ref[0, g, 0:t, :]) for g in range(ng)]
    o_ref[0] = jnp.concatenate(heads, axis=1).astype(BF16)


def _sb_prompt(q, k, v):
    nb, nh, seq, dh = v.shape
    t, ng = SB_TILE, SB_HEAD_GROUP
    kv_spec = lambda w: pl.BlockSpec((1, ng, seq, w), lambda b, h, i: (b, h, 0, 0),
                                     pipeline_mode=pl.Buffered(1))
    qo_spec = lambda w: pl.BlockSpec((1, ng, t, w), lambda b, h, i: (b, h, i, 0))
    return pl.pallas_call(
        _sb_prompt_kernel,
        grid=(nb, nh // ng, seq // t),
        in_specs=[qo_spec(q.shape[3]), kv_spec(k.shape[3]), kv_spec(dh), _resident((t, t))],
        out_specs=pl.BlockSpec((1, t, ng * dh), lambda b, h, i: (b, i, h)),
        out_shape=jax.ShapeDtypeStruct((nb, seq, nh * dh), BF16),
        scratch_shapes=[pltpu.VMEM((ng, t, t), F32),
                        pltpu.VMEM((ng, t, t), BF16), pltpu.VMEM((ng, t, t), F32),
                        pltpu.VMEM((ng, t, t), BF16), pltpu.VMEM((ng, t, dh), F32),
                        pltpu.VMEM((ng, t, 1), F32)],
        compiler_params=_params("arbitrary", "arbitrary", "arbitrary"),
        name="sb_prompt",
    )(q, k, v, _neg_upper(t))


def _merge_kernel(x_ref, ysb_ref, yconv_ref, yxa_ref, gpre_ref, wgate_ref, bgate_ref,
                  wsb_ref, wcv_ref, wxa_ref, wo_ref, gpost_ref, h_ref):
    x = x_ref[0]
    xn = _rms(x, gpre_ref[...]).astype(BF16)

    def gate(br):
        sl = slice(br * D_MODEL, (br + 1) * D_MODEL)
        return 1.0 / (1.0 + jnp.exp(-(_dot(xn, wgate_ref[:, sl]) + bgate_ref[:, sl])))

    m = gate(0) * _dot(ysb_ref[0], wsb_ref[...])
    m = m + gate(1) * _dot(yconv_ref[0], wcv_ref[...])
    m = m + gate(2) * _dot(yxa_ref[0], wxa_ref[...])
    mo = _dot(m.astype(BF16), wo_ref[...])
    h_ref[0] = x + _rms(mo, gpost_ref[...])


def _merge(x, ysb, yconv, yxa, g_pre, w_gate_b, b_gate, w_sb_b, w_cv_b, w_xa_b, w_o_b, g_post, tm):
    nb, seq, _ = x.shape
    row_spec = lambda w: pl.BlockSpec((1, tm, w), lambda b, i: (b, i, 0))
    consts = (g_pre, w_gate_b, b_gate, w_sb_b, w_cv_b, w_xa_b, w_o_b, g_post)
    return pl.pallas_call(
        _merge_kernel,
        grid=(nb, seq // tm),
        in_specs=[row_spec(D_MODEL), row_spec(SB_WIDTH), row_spec(CONV_WIDTH), row_spec(XA_WIDTH)]
                 + [_resident(c.shape) for c in consts],
        out_specs=row_spec(D_MODEL),
        out_shape=jax.ShapeDtypeStruct((nb, seq, D_MODEL), F32),
        compiler_params=_params("arbitrary", "arbitrary"),
        name="merge",
    )(x, ysb, yconv, yxa, *consts)


def _ffn_tile(h_ref, gpre_ref, wup_ref, wdown_ref, gpost_ref, o_ref, side):
    h = h_ref[0]
    hn = _rms(h, gpre_ref[...]).astype(BF16)
    f = jnp.zeros(h.shape, F32)
    for c in range(HOST_CHUNKS):
        sl = slice(c * FF_CHUNK, (c + 1) * FF_CHUNK)
        a = jnp.maximum(_dot(hn, wup_ref[:, sl]), 0.0)
        f = f + _dot((a * a).astype(BF16), wdown_ref[sl, :])
        side.chunk(c)
        side.advance(c)
    o_ref[0] = h + _rms(f, gpost_ref[...])


def _ffn_kernel(h_ref, gpre_ref, wup_ref, wdown_ref, gpost_ref, o_ref):
    _ffn_tile(h_ref, gpre_ref, wup_ref, wdown_ref, gpost_ref, o_ref, _NoSideWork())


def _ffn(h, g_pre, w_up_b, w_down_b, g_post, tm):
    nb, seq, _ = h.shape
    row_spec = pl.BlockSpec((1, tm, D_MODEL), lambda b, i: (b, i, 0))
    consts = (g_pre, w_up_b, w_down_b, g_post)
    return pl.pallas_call(
        _ffn_kernel,
        grid=(nb, seq // tm),
        in_specs=[row_spec] + [_resident(c.shape) for c in consts],
        out_specs=row_spec,
        out_shape=jax.ShapeDtypeStruct((nb, seq, D_MODEL), F32),
        compiler_params=_params("arbitrary", "arbitrary"),
        name="ffn",
    )(h, *consts)


class _NoSideWork:
    def chunk(self, c):
        pass

    def advance(self, c):
        pass


class _HostedDecode:
    def __init__(self, pt_ref, in_refs, out_and_scratch, n):
        self.pt = pt_ref
        self.q, self.bias, self.kt, self.vt = in_refs
        self.ysb, self.kbuf, self.vbuf, self.sems, self.acc, self.carry = out_and_scratch
        self.n = n
        self.n_pages = pt_ref.shape[1]
        self.steps_per_seq = self.n_pages // (HOST_CHUNKS * n)
        self.step = pl.program_id(0) * pl.num_programs(1) + pl.program_id(1)
        self.n_steps = pl.num_programs(0) * pl.num_programs(1)
        self.part = self.step % self.steps_per_seq

    def _copies(self, steps_ahead, c):
        step = self.step + steps_ahead
        seq = step // self.steps_per_seq
        first = ((step % self.steps_per_seq) * HOST_CHUNKS + c) * self.n
        slot = c % DECODE_RING
        out = []
        for p in range(self.n):
            page = self.pt[seq, self.n_pages - 1 - (first + p)]
            out.append(pltpu.make_async_copy(self.kt.at[page], self.kbuf.at[slot, p],
                                             self.sems.at[0, slot, p]))
            out.append(pltpu.make_async_copy(self.vt.at[page], self.vbuf.at[slot, p],
                                             self.sems.at[1, slot, p]))
        return out

    def begin(self):
        @pl.when(self.step == 0)
        def _():
            for c in range(DECODE_RING):
                for cp in self._copies(0, c):
                    cp.start()

        @pl.when(self.part == 0)
        def _():
            self.acc[...] = jnp.zeros(self.acc.shape, F32)
            self.carry[...] = jnp.zeros(self.carry.shape, F32)

        for cp in self._copies(0, 0):
            cp.wait()

    def chunk(self, c):
        slot = c % DECODE_RING
        ws, carry = _decode_weights(self.q[0], self.bias[...],
                                    [self.kbuf[slot, p] for p in range(self.n)], self.carry[...])
        self.carry[...] = carry
        for h in range(SB_HEADS):
            acc = self.acc[h]
            for p in range(self.n):
                acc = acc + ws[p][h:h + 1, :] * self.vbuf[slot, p, h]
            self.acc[h] = acc

    def advance(self, c):
        steps_ahead, c_ahead = divmod(c + DECODE_RING, HOST_CHUNKS)
        if steps_ahead == 0:
            for cp in self._copies(0, c_ahead):
                cp.start()
        else:
            @pl.when(self.step + steps_ahead < self.n_steps)
            def _():
                for cp in self._copies(steps_ahead, c_ahead):
                    cp.start()
        if c + 1 < HOST_CHUNKS:
            for cp in self._copies(0, c + 1):
                cp.wait()

    def finish(self):
        @pl.when(self.part == self.steps_per_seq - 1)
        def _():
            self.ysb[0] = jnp.sum(self.acc[...], axis=2, keepdims=True)


class _HostedDecodeSetup:
    pass


def _hosted_decode_setup(decode_args, n, nb, steps_per_b):
    page_table, q, b_sb, kt_pages, vt_pages = decode_args
    n_pages = page_table.shape[1]
    pages_per_step = HOST_CHUNKS * n
    assert n_pages % pages_per_step == 0 and HOST_CHUNKS % DECODE_RING == 0
    steps_per_seq = n_pages // pages_per_step
    n_hosted = page_table.shape[0]
    assert n_hosted * steps_per_seq == nb * steps_per_b, "host grid must cover every decode sequence"
    page_shape = (SB_HEADS, SB_HEAD_DIM, PAGE_SIZE)
    bias = jnp.tile(b_sb.reshape(SB_HEADS, 1) * LOG2E, (n, 1))
    seq_spec = pl.BlockSpec((1, SB_HEADS, SB_HEAD_DIM, 1),
                            lambda b, i, pt: ((b * steps_per_b + i) // steps_per_seq, 0, 0, 0))
    host = _HostedDecodeSetup()
    host.page_table = page_table
    host.operands = (q.reshape(n_hosted, SB_HEADS, SB_HEAD_DIM, 1), bias, kt_pages, vt_pages)
    host.in_specs = [seq_spec, _resident(bias.shape),
                     pl.BlockSpec(memory_space=pl.ANY), pl.BlockSpec(memory_space=pl.ANY)]
    host.out_spec = seq_spec
    host.out_shape = jax.ShapeDtypeStruct((n_hosted, SB_HEADS, SB_HEAD_DIM, 1), F32)
    host.scratch_shapes = [pltpu.VMEM((DECODE_RING, n) + page_shape, F32),
                           pltpu.VMEM((DECODE_RING, n) + page_shape, F32),
                           pltpu.SemaphoreType.DMA((2, DECODE_RING, n)),
                           pltpu.VMEM(page_shape, F32),
                           pltpu.VMEM((SB_HEADS, 1), F32)]
    return host


def _ffn_decode_kernel(pt_ref, h_ref, gpre_ref, wup_ref, wdown_ref, gpost_ref, *rest):
    o_ref = rest[HOSTED_INPUTS]
    side = _HostedDecode(pt_ref, rest[:HOSTED_INPUTS], rest[HOSTED_INPUTS + 1:], FFN_DECODE_PAGES)
    side.begin()
    _ffn_tile(h_ref, gpre_ref, wup_ref, wdown_ref, gpost_ref, o_ref, side)
    side.finish()


def _ffn_decode(h, g_pre, w_up_b, w_down_b, g_post, tm, decode_args):
    nb, seq, _ = h.shape
    row_spec = pl.BlockSpec((1, tm, D_MODEL), lambda b, i, pt: (b, i, 0))
    consts = (g_pre, w_up_b, w_down_b, g_post)
    host = _hosted_decode_setup(decode_args, FFN_DECODE_PAGES, nb, seq // tm)
    grid_spec = pltpu.PrefetchScalarGridSpec(
        num_scalar_prefetch=1,
        grid=(nb, seq // tm),
        in_specs=[row_spec] + [_resident(c.shape) for c in consts] + host.in_specs,
        out_specs=(row_spec, host.out_spec),
        scratch_shapes=host.scratch_shapes,
    )
    return pl.pallas_call(
        _ffn_decode_kernel,
        grid_spec=grid_spec,
        out_shape=(jax.ShapeDtypeStruct((nb, seq, D_MODEL), F32), host.out_shape),
        compiler_params=_params("arbitrary", "arbitrary"),
        name="ffn_decode",
    )(host.page_table, h, *consts, *host.operands)


def _proj_sample_kernel(x_ref, g_ref, win_ref, wconv_ref, s0_ref, s1_ref,
                        q_ref, k_ref, v_ref, yconv_ref, c_ref, xq_ref):
    xn = _rms(x_ref[...], g_ref[...]).astype(BF16)

    def proj(c):
        return _dot(xn, win_ref[:, c * IN_CHUNK:(c + 1) * IN_CHUNK])

    q_ref[...] = proj(0) * SB_QSCALE
    k_ref[...] = proj(1)
    v_ref[...] = proj(2)
    cb = proj(3)
    c = proj(4) * proj(5)
    w = wconv_ref[...]
    yconv_ref[...] = (cb * (s0_ref[...] * w[0:1] + s1_ref[...] * w[1:2] + c * w[2:3])).astype(BF16)
    c_ref[...] = c
    xq_ref[...] = proj(6) * XA_SCALE


def _proj_sample(x2d, g, w_in_b, w_conv, s0, s1):
    n = x2d.shape[0]
    wide = jax.ShapeDtypeStruct((n, IN_CHUNK), F32)
    return pl.pallas_call(
        _proj_sample_kernel,
        out_shape=(wide, wide, wide, jax.ShapeDtypeStruct((n, CONV_WIDTH), BF16), wide, wide),
        compiler_params=pltpu.CompilerParams(vmem_limit_bytes=VMEM_LIMIT),
        name="proj_sample",
    )(x2d, g, w_in_b, w_conv, s0, s1)


def _xattn_sample_kernel(xq_ref, mk_ref, mv_ref, o_ref):
    for s in range(xq_ref.shape[0]):
        xq = xq_ref[s]
        for h in range(XA_HEADS):
            sl = slice(h * XA_HEAD_DIM, (h + 1) * XA_HEAD_DIM)
            head_rows = pl.ds(h, N_MEM, stride=XA_HEADS)
            qh = jnp.broadcast_to(xq[:, sl], (8, XA_HEAD_DIM)).astype(BF16)
            sc = _dot_nt(qh, mk_ref[s, head_rows, :].astype(BF16))
            e = jnp.exp(sc - jnp.max(sc, axis=-1, keepdims=True))
            o = _dot(e.astype(BF16), mv_ref[s, head_rows, :].astype(BF16)) / jnp.sum(e, axis=-1, keepdims=True)
            o_ref[s, :, sl] = o[0:1].astype(BF16)


def _xattn_sample(xq, mk, mv):
    n = xq.shape[0]
    nblk = XA_SAMPLE_BLOCK
    q_spec = pl.BlockSpec((nblk, 1, XA_WIDTH), lambda b: (b, 0, 0))
    m_spec = pl.BlockSpec((nblk, N_MEM * XA_HEADS, XA_HEAD_DIM), lambda b: (b, 0, 0))
    return pl.pallas_call(
        _xattn_sample_kernel,
        grid=(n // nblk,),
        in_specs=[q_spec, m_spec, m_spec],
        out_specs=q_spec,
        out_shape=jax.ShapeDtypeStruct((n, 1, XA_WIDTH), BF16),
        compiler_params=_params("arbitrary"),
        name="xattn_sample",
    )(xq.reshape(n, 1, XA_WIDTH), mk, mv)


def _decode_weights(qb, bias, k_pages, carry):
    z = jnp.concatenate([jnp.sum(qb * kp, axis=1) for kp in k_pages], axis=0) + bias
    p_all = _softplus2(z)
    lane = lax.broadcasted_iota(jnp.int32, p_all.shape, 1)
    suffix = p_all
    shift = 1
    while shift < PAGE_SIZE:
        moved = pltpu.roll(suffix, PAGE_SIZE - shift, axis=1)
        suffix = suffix + jnp.where(lane < PAGE_SIZE - shift, moved, 0.0)
        shift *= 2
    ws = []
    for p in range(len(k_pages)):
        sl = slice(p * SB_HEADS, (p + 1) * SB_HEADS)
        ws.append(jnp.exp2((z[sl] - suffix[sl]) + carry))
        carry = carry - suffix[sl][:, 0:1]
    return ws, carry


def kernel(x_prompt, x_sample, mem_prompt, cache_k_pages, cache_v_pages, page_table, cache_mem_k,
           cache_mem_v, state_conv, g_mix_pre, w_in, b_sb, w_conv, g_mem, w_mem_kv, w_gate, b_gate,
           w_sb_o, w_conv_o, w_xa_o, w_o, g_mix_post, g_ffn_pre, w_up, w_down, g_ffn_post):
    depth = w_in.shape[0]
    assert depth == 1, "single-layer step"
    nb, seq, _ = x_prompt.shape
    nseq = x_sample.shape[0]
    bf = lambda a: a[0].astype(BF16)
    w_in_b, w_gate_b, w_mem_b = bf(w_in), bf(w_gate), bf(w_mem_kv)
    w_sb_b, w_cv_b, w_xa_b, w_o_b = bf(w_sb_o), bf(w_conv_o), bf(w_xa_o), bf(w_o)
    w_up_b, w_down_b = bf(w_up), bf(w_down)
    tail = (g_mix_pre, w_gate_b, b_gate, w_sb_b, w_cv_b, w_xa_b, w_o_b, g_mix_post)
    ffn_w = (g_ffn_pre, w_up_b, w_down_b, g_ffn_post)

    xs = x_sample.reshape(nseq, D_MODEL)
    q_s, k_s, v_s, yconv_s, c_s, xq_s = _proj_sample(
        xs, g_mix_pre, w_in_b, w_conv[0], state_conv[0, :, 0, :], state_conv[0, :, 1, :])
    kt_pages = jnp.transpose(cache_k_pages[0], (0, 2, 3, 1))
    vt_pages = jnp.transpose(cache_v_pages[0], (0, 2, 3, 1))

    kv, kv_b = _memkv(mem_prompt.reshape(nb * N_MEM, D_MODEL), g_mem, w_mem_b)
    mk = kv[:, :XA_WIDTH].reshape(1, nb, N_MEM, XA_HEADS, XA_HEAD_DIM)
    mv = kv[:, XA_WIDTH:].reshape(1, nb, N_MEM, XA_HEADS, XA_HEAD_DIM)
    mk_b = kv_b[:, :XA_WIDTH].reshape(nb, N_MEM, XA_WIDTH)
    mv_b = kv_b[:, XA_WIDTH:].reshape(nb, N_MEM, XA_WIDTH)
    q_p, k_p, v_p, kt, vt, yconv_p, cstate_p, yxa_p = _proj_prompt(
        x_prompt, g_mix_pre, w_in_b, w_conv[0], mk_b, mv_b, _bias_columns(b_sb[0]))
    ysb_p = _sb_prompt(q_p, k_p, v_p)
    h_p = _merge(x_prompt, ysb_p, yconv_p, yxa_p, *tail, tm=PROMPT_TILE)
    y_p, ysb_s = _ffn_decode(h_p, *ffn_w, PROMPT_TILE, (page_table, q_s, b_sb[0], kt_pages, vt_pages))
    to_cache = lambda a: jnp.transpose(
        a.reshape(nb, SB_HEADS, SB_HEAD_DIM, seq), (0, 3, 1, 2))[None]
    k_prompt, v_prompt = to_cache(kt), to_cache(vt)

    ysb_s = ysb_s.reshape(1, nseq, SB_WIDTH).astype(BF16)
    yxa_s = _xattn_sample(xq_s, cache_mem_k[0].reshape(nseq, N_MEM * XA_HEADS, XA_HEAD_DIM),
                          cache_mem_v[0].reshape(nseq, N_MEM * XA_HEADS, XA_HEAD_DIM))
    h_s = _merge(xs[None], ysb_s, yconv_s[None], yxa_s.reshape(1, nseq, XA_WIDTH), *tail, tm=nseq)
    y_s = _ffn(h_s, *ffn_w, tm=nseq)
    conv_sample = jnp.stack([state_conv[0, :, 1, :], c_s], axis=1)[None]
    to_tok = lambda a: a.reshape(1, nseq, 1, SB_HEADS, SB_HEAD_DIM)

    return (y_p, y_s.reshape(nseq, 1, D_MODEL), k_prompt, v_prompt, cstate_p[None], mk, mv,
            to_tok(k_s), to_tok(v_s), conv_sample)
```

```python
import math

import jax
import jax.numpy as jnp
from jax import lax
from jax.experimental import pallas as pl
from jax.experimental.pallas import tpu as pltpu

F32 = jnp.float32
BF16 = jnp.bfloat16

D_MODEL = 1024
SB_HEADS = 8
SB_HEAD_DIM = 64
SB_WIDTH = SB_HEADS * SB_HEAD_DIM
CONV_WIDTH = 512
CONV_K = 3
XA_HEADS = 4
XA_HEAD_DIM = 128
XA_WIDTH = XA_HEADS * XA_HEAD_DIM
N_MEM = 256
D_FF = 4 * D_MODEL
RMS_EPS = 1e-6
PAGE_SIZE = 128
IN_CHUNK = 512
LOG2E = math.log2(math.e)
SB_QSCALE = LOG2E / math.sqrt(SB_HEAD_DIM)
BIAS_TERMS = 3
MASKED_LOG2 = -1e30
XA_SCALE = 1.0 / math.sqrt(XA_HEAD_DIM)

V7X_VMEM_BYTES = 64 * 1024 * 1024
VMEM_LIMIT = V7X_VMEM_BYTES - 8 * 1024 * 1024

PROMPT_TILE = 512
SB_TILE = 256
SB_HEAD_GROUP = 4
FF_CHUNK = 256
HOST_CHUNKS = D_FF // FF_CHUNK
FFN_DECODE_PAGES = 8
XA_SAMPLE_BLOCK = 8
HOSTED_INPUTS = 4
DECODE_RING = 4


def _params(*sem):
    return pltpu.CompilerParams(dimension_semantics=sem, vmem_limit_bytes=VMEM_LIMIT)


def _resident(shape):
    nd = len(shape)
    return pl.BlockSpec(shape, lambda *_: (0,) * nd, pipeline_mode=pl.Buffered(1))


def _rms(x, g):
    return x * lax.rsqrt(jnp.mean(x * x, axis=-1, keepdims=True) + RMS_EPS) * g


def _dot(a, b):
    return jnp.dot(a, b, preferred_element_type=F32)


def _dot_nt(a, b):
    return lax.dot_general(a, b, (((1,), (1,)), ((), ())), preferred_element_type=F32)


def _softplus2(z2):
    return jnp.maximum(z2, jnp.log2(1.0 + jnp.exp2(jnp.minimum(z2, 64.0))))


def _bias_columns(b_sb):
    rest = b_sb.astype(F32) * LOG2E
    cols = []
    for _ in range(BIAS_TERMS):
        piece = rest.astype(BF16).astype(F32)
        cols.append(piece)
        rest = rest - piece
    pad = jnp.zeros((b_sb.shape[0], SB_HEAD_DIM - BIAS_TERMS), F32)
    return jnp.concatenate([jnp.stack(cols, axis=1), pad], axis=1)


def _neg_upper(n):
    s = lax.broadcasted_iota(jnp.int32, (n, n), 0)
    j = lax.broadcasted_iota(jnp.int32, (n, n), 1)
    return jnp.where(s > j, -1.0, 0.0).astype(BF16)


def _memkv_kernel(mem_ref, g_ref, w_ref, kv_ref, kvb_ref):
    mn = _rms(mem_ref[...], g_ref[...]).astype(BF16)
    kv = _dot(mn, w_ref[...])
    kv_ref[...] = kv
    kvb_ref[...] = kv.astype(BF16)


def _memkv(mem2d, g_mem, w_mem_kv_b):
    rows = mem2d.shape[0]
    return pl.pallas_call(
        _memkv_kernel,
        out_shape=(jax.ShapeDtypeStruct((rows, 2 * XA_WIDTH), F32),
                   jax.ShapeDtypeStruct((rows, 2 * XA_WIDTH), BF16)),
        compiler_params=pltpu.CompilerParams(vmem_limit_bytes=VMEM_LIMIT),
        name="memkv",
    )(mem2d, g_mem, w_mem_kv_b)


def _proj_prompt_kernel(x_ref, g_ref, win_ref, wconv_ref, mk_ref, mv_ref, kpad_ref,
                        q_ref, k_ref, v_ref, kt_ref, vt_ref, yconv_ref, cstate_ref, yxa_ref,
                        hist_ref):
    tm = x_ref.shape[1]
    i = pl.program_id(1)
    xn = _rms(x_ref[0], g_ref[...]).astype(BF16)

    def proj(c):
        return _dot(xn, win_ref[:, c * IN_CHUNK:(c + 1) * IN_CHUNK])

    q = proj(0) * SB_QSCALE
    k = proj(1)
    v = proj(2)
    lane = lax.broadcasted_iota(jnp.int32, (tm, SB_HEAD_DIM), 1)
    qpad = jnp.where(lane < BIAS_TERMS, 1.0, 0.0)
    for h in range(SB_HEADS):
        sl = slice(h * SB_HEAD_DIM, (h + 1) * SB_HEAD_DIM)
        kpad = jnp.broadcast_to(kpad_ref[h:h + 1, :], (tm, SB_HEAD_DIM))
        q_ref[0, h] = jnp.concatenate([q[:, sl], qpad], axis=1).astype(BF16)
        k_ref[0, h] = jnp.concatenate([k[:, sl], kpad], axis=1).astype(BF16)
        v_ref[0, h] = v[:, sl].astype(BF16)
    kt_ref[0] = k.T
    vt_ref[0] = v.T

    cb = proj(3)
    c = proj(4) * proj(5)

    @pl.when(i == 0)
    def _():
        hist_ref[0:8, :] = jnp.zeros((8, CONV_WIDTH), F32)

    hist_ref[8:8 + tm, :] = c
    c_m1 = hist_ref[pl.ds(7, tm), :]
    c_m2 = hist_ref[pl.ds(6, tm), :]
    w = wconv_ref[...]
    yconv_ref[0] = (cb * (c_m2 * w[0:1] + c_m1 * w[1:2] + c * w[2:3])).astype(BF16)
    tail = c[tm - (CONV_K - 1):tm, :]
    hist_ref[6:8, :] = tail
    cstate_ref[0] = tail

    xq = proj(6) * XA_SCALE
    for h in range(XA_HEADS):
        sl = slice(h * XA_HEAD_DIM, (h + 1) * XA_HEAD_DIM)
        s = _dot_nt(xq[:, sl].astype(BF16), mk_ref[0, :, sl])
        e = jnp.exp(s - jnp.max(s, axis=-1, keepdims=True))
        o = _dot(e.astype(BF16), mv_ref[0, :, sl]) / jnp.sum(e, axis=-1, keepdims=True)
        yxa_ref[0, :, sl] = o.astype(BF16)


def _proj_prompt(x, g, w_in_b, w_conv, mk_b, mv_b, kpad):
    nb, seq, _ = x.shape
    tm = PROMPT_TILE
    head_spec = pl.BlockSpec((1, SB_HEADS, tm, SB_HEAD_DIM), lambda b, i: (b, 0, i, 0))
    wide_spec = pl.BlockSpec((1, SB_HEADS, tm, 2 * SB_HEAD_DIM), lambda b, i: (b, 0, i, 0))
    t_spec = pl.BlockSpec((1, SB_WIDTH, tm), lambda b, i: (b, 0, i))
    row_spec = lambda w: pl.BlockSpec((1, tm, w), lambda b, i: (b, i, 0))
    mem_spec = pl.BlockSpec((1, N_MEM, XA_WIDTH), lambda b, i: (b, 0, 0))
    head_shape = jax.ShapeDtypeStruct((nb, SB_HEADS, seq, SB_HEAD_DIM), BF16)
    wide_shape = jax.ShapeDtypeStruct((nb, SB_HEADS, seq, 2 * SB_HEAD_DIM), BF16)
    t_shape = jax.ShapeDtypeStruct((nb, SB_WIDTH, seq), F32)
    return pl.pallas_call(
        _proj_prompt_kernel,
        grid=(nb, seq // tm),
        in_specs=[row_spec(D_MODEL), _resident(g.shape), _resident(w_in_b.shape),
                  _resident(w_conv.shape), mem_spec, mem_spec, _resident(kpad.shape)],
        out_specs=(wide_spec, wide_spec, head_spec, t_spec, t_spec, row_spec(CONV_WIDTH),
                   pl.BlockSpec((1, CONV_K - 1, CONV_WIDTH), lambda b, i: (b, 0, 0)),
                   row_spec(XA_WIDTH)),
        out_shape=(wide_shape, wide_shape, head_shape, t_shape, t_shape,
                   jax.ShapeDtypeStruct((nb, seq, CONV_WIDTH), BF16),
                   jax.ShapeDtypeStruct((nb, CONV_K - 1, CONV_WIDTH), F32),
                   jax.ShapeDtypeStruct((nb, seq, XA_WIDTH), BF16)),
        scratch_shapes=[pltpu.VMEM((tm + 8, CONV_WIDTH), F32)],
        compiler_params=_params("arbitrary", "arbitrary"),
        name="proj_prompt",
    )(x, g, w_in_b, w_conv, mk_b, mv_b, kpad)


def _sb_prompt_kernel(q_ref, k_ref, v_ref, negu_ref, o_ref, z_scr, p_scr, zmp_scr, w_scr, acc_scr,
                      carry_scr):
    ng, t = q_ref.shape[1], q_ref.shape[2]
    i = pl.program_id(2)
    negu = negu_ref[...]

    def rows(j):
        return pl.ds(pl.multiple_of(j * t, t), t)

    def logits(g, j):
        return _dot_nt(q_ref[0, g], k_ref[0, g, rows(j), :])

    def keep_and_beta(z):
        p = _softplus2(z)
        return p, z - p

    row = lax.broadcasted_iota(jnp.int32, (t, t), 0)
    col = lax.broadcasted_iota(jnp.int32, (t, t), 1)
    causal = col < row
    for g in range(ng):
        p, zmp = keep_and_beta(logits(g, i))
        p_scr[g] = jnp.where(causal, p, 0.0).astype(BF16)
        zmp_scr[g] = jnp.where(causal, zmp, MASKED_LOG2)
        z_scr[g] = logits(g, jnp.maximum(i - 1, 0))
        w_scr[g] = jnp.zeros((t, t), BF16)
        acc_scr[g] = jnp.zeros((t, SB_HEAD_DIM), F32)
        carry_scr[g] = jnp.zeros((t, 1), F32)

    def trip(n, _):
        j_prev = jnp.minimum(i - n + 1, i)
        j_next2 = jnp.maximum(i - n - 2, 0)
        for g in range(ng):
            acc_scr[g] += _dot(w_scr[g], v_ref[0, g, rows(j_prev), :])
            p_cur = p_scr[g]
            after = _dot(p_cur, negu)
            keep = after + carry_scr[g]
            w = jnp.exp2(zmp_scr[g] + keep)
            carry_scr[g] = keep[:, 0:1] - p_cur[:, 0:1].astype(F32)
            p, zmp = keep_and_beta(z_scr[g])
            z_scr[g] = logits(g, j_next2)
            w_scr[g] = w.astype(BF16)
            p_scr[g] = p.astype(BF16)
            zmp_scr[g] = zmp
        return 0

    lax.fori_loop(0, i + 1, trip, 0)
    heads = [acc_scr[g] + _dot(w_scr[g], v_ref[0, g, 0:t, :]) for g in range(ng)]
    o_ref[0] = jnp.concatenate(heads, axis=1).astype(BF16)


def _sb_prompt(q, k, v):
    nb, nh, seq, dh = v.shape
    t, ng = SB_TILE, SB_HEAD_GROUP
    kv_spec = lambda w: pl.BlockSpec((1, ng, seq, w), lambda b, h, i: (b, h, 0, 0))
    qo_spec = lambda w: pl.BlockSpec((1, ng, t, w), lambda b, h, i: (b, h, i, 0))
    return pl.pallas_call(
        _sb_prompt_kernel,
        grid=(nb, nh // ng, seq // t),
        in_specs=[qo_spec(q.shape[3]), kv_spec(k.shape[3]), kv_spec(dh), _resident((t, t))],
        out_specs=pl.BlockSpec((1, t, ng * dh), lambda b, h, i: (b, i, h)),
        out_shape=jax.ShapeDtypeStruct((nb, seq, nh * dh), BF16),
        scratch_shapes=[pltpu.VMEM((ng, t, t), F32),
                        pltpu.VMEM((ng, t, t), BF16), pltpu.VMEM((ng, t, t), F32),
                        pltpu.VMEM((ng, t, t), BF16), pltpu.VMEM((ng, t, dh), F32),
                        pltpu.VMEM((ng, t, 1), F32)],
        compiler_params=_params("arbitrary", "arbitrary", "arbitrary"),
        name="sb_prompt",
    )(q, k, v, _neg_upper(t))


def _merge_kernel(x_ref, ysb_ref, yconv_ref, yxa_ref, gpre_ref, wgate_ref, bgate_ref,
                  wsb_ref, wcv_ref, wxa_ref, wo_ref, gpost_ref, h_ref):
    x = x_ref[0]
    xn = _rms(x, gpre_ref[...]).astype(BF16)

    def gate(br):
        sl = slice(br * D_MODEL, (br + 1) * D_MODEL)
        return 1.0 / (1.0 + jnp.exp(-(_dot(xn, wgate_ref[:, sl]) + bgate_ref[:, sl])))

    m = gate(0) * _dot(ysb_ref[0], wsb_ref[...])
    m = m + gate(1) * _dot(yconv_ref[0], wcv_ref[...])
    m = m + gate(2) * _dot(yxa_ref[0], wxa_ref[...])
    mo = _dot(m.astype(BF16), wo_ref[...])
    h_ref[0] = x + _rms(mo, gpost_ref[...])


def _merge(x, ysb, yconv, yxa, g_pre, w_gate_b, b_gate, w_sb_b, w_cv_b, w_xa_b, w_o_b, g_post, tm):
    nb, seq, _ = x.shape
    row_spec = lambda w: pl.BlockSpec((1, tm, w), lambda b, i: (b, i, 0))
    consts = (g_pre, w_gate_b, b_gate, w_sb_b, w_cv_b, w_xa_b, w_o_b, g_post)
    return pl.pallas_call(
        _merge_kernel,
        grid=(nb, seq // tm),
        in_specs=[row_spec(D_MODEL), row_spec(SB_WIDTH), row_spec(CONV_WIDTH), row_spec(XA_WIDTH)]
                 + [_resident(c.shape) for c in consts],
        out_specs=row_spec(D_MODEL),
        out_shape=jax.ShapeDtypeStruct((nb, seq, D_MODEL), F32),
        compiler_params=_params("arbitrary", "arbitrary"),
        name="merge",
    )(x, ysb, yconv, yxa, *consts)


def _ffn_tile(h_ref, gpre_ref, wup_ref, wdown_ref, gpost_ref, o_ref, side):
    h = h_ref[0]
    hn = _rms(h, gpre_ref[...]).astype(BF16)
    f = jnp.zeros(h.shape, F32)
    for c in range(HOST_CHUNKS):
        sl = slice(c * FF_CHUNK, (c + 1) * FF_CHUNK)
        a = jnp.maximum(_dot(hn, wup_ref[:, sl]), 0.0)
        f = f + _dot((a * a).astype(BF16), wdown_ref[sl, :])
        side.chunk(c)
        side.advance(c)
    o_ref[0] = h + _rms(f, gpost_ref[...])


def _ffn_kernel(h_ref, gpre_ref, wup_ref, wdown_ref, gpost_ref, o_ref):
    _ffn_tile(h_ref, gpre_ref, wup_ref, wdown_ref, gpost_ref, o_ref, _NoSideWork())


def _ffn(h, g_pre, w_up_b, w_down_b, g_post, tm):
    nb, seq, _ = h.shape
    row_spec = pl.BlockSpec((1, tm, D_MODEL), lambda b, i: (b, i, 0))
    consts = (g_pre, w_up_b, w_down_b, g_post)
    return pl.pallas_call(
        _ffn_kernel,
        grid=(nb, seq // tm),
        in_specs=[row_spec] + [_resident(c.shape) for c in consts],
        out_specs=row_spec,
        out_shape=jax.ShapeDtypeStruct((nb, seq, D_MODEL), F32),
        compiler_params=_params("arbitrary", "arbitrary"),
        name="ffn",
    )(h, *consts)


class _NoSideWork:
    def chunk(self, c):
        pass

    def advance(self, c):
        pass


class _HostedDecode:
    def __init__(self, pt_ref, in_refs, out_and_scratch, n):
        self.pt = pt_ref
        self.q, self.bias, self.kt, self.vt = in_refs
        self.ysb, self.kbuf, self.vbuf, self.sems, self.acc, self.carry = out_and_scratch
        self.n = n
        self.n_pages = pt_ref.shape[1]
        self.steps_per_seq = self.n_pages // (HOST_CHUNKS * n)
        self.step = pl.program_id(0) * pl.num_programs(1) + pl.program_id(1)
        self.n_steps = pl.num_programs(0) * pl.num_programs(1)
        self.part = self.step % self.steps_per_seq

    def _copies(self, steps_ahead, c):
        step = self.step + steps_ahead
        seq = step // self.steps_per_seq
        first = ((step % self.steps_per_seq) * HOST_CHUNKS + c) * self.n
        slot = c % DECODE_RING
        out = []
        for p in range(self.n):
            page = self.pt[seq, self.n_pages - 1 - (first + p)]
            out.append(pltpu.make_async_copy(self.kt.at[page], self.kbuf.at[slot, p],
                                             self.sems.at[0, slot, p]))
            out.append(pltpu.make_async_copy(self.vt.at[page], self.vbuf.at[slot, p],
                                             self.sems.at[1, slot, p]))
        return out

    def begin(self):
        @pl.when(self.step == 0)
        def _():
            for c in range(DECODE_RING):
                for cp in self._copies(0, c):
                    cp.start()

        @pl.when(self.part == 0)
        def _():
            self.acc[...] = jnp.zeros(self.acc.shape, F32)
            self.carry[...] = jnp.zeros(self.carry.shape, F32)

        for cp in self._copies(0, 0):
            cp.wait()

    def chunk(self, c):
        slot = c % DECODE_RING
        ws, carry = _decode_weights(self.q[0], self.bias[...],
                                    [self.kbuf[slot, p] for p in range(self.n)], self.carry[...])
        self.carry[...] = carry
        for h in range(SB_HEADS):
            acc = self.acc[h]
            for p in range(self.n):
                acc = acc + ws[p][h:h + 1, :] * self.vbuf[slot, p, h]
            self.acc[h] = acc

    def advance(self, c):
        steps_ahead, c_ahead = divmod(c + DECODE_RING, HOST_CHUNKS)
        if steps_ahead == 0:
            for cp in self._copies(0, c_ahead):
                cp.start()
        else:
            @pl.when(self.step + steps_ahead < self.n_steps)
            def _():
                for cp in self._copies(steps_ahead, c_ahead):
                    cp.start()
        if c + 1 < HOST_CHUNKS:
            for cp in self._copies(0, c + 1):
                cp.wait()

    def finish(self):
        @pl.when(self.part == self.steps_per_seq - 1)
        def _():
            self.ysb[0] = jnp.sum(self.acc[...], axis=2, keepdims=True)


class _HostedDecodeSetup:
    pass


def _hosted_decode_setup(decode_args, n, nb, steps_per_b):
    page_table, q, b_sb, kt_pages, vt_pages = decode_args
    n_pages = page_table.shape[1]
    pages_per_step = HOST_CHUNKS * n
    assert n_pages % pages_per_step == 0 and HOST_CHUNKS % DECODE_RING == 0
    steps_per_seq = n_pages // pages_per_step
    n_hosted = page_table.shape[0]
    assert n_hosted * steps_per_seq == nb * steps_per_b, "host grid must cover every decode sequence"
    page_shape = (SB_HEADS, SB_HEAD_DIM, PAGE_SIZE)
    bias = jnp.tile(b_sb.reshape(SB_HEADS, 1) * LOG2E, (n, 1))
    seq_spec = pl.BlockSpec((1, SB_HEADS, SB_HEAD_DIM, 1),
                            lambda b, i, pt: ((b * steps_per_b + i) // steps_per_seq, 0, 0, 0))
    host = _HostedDecodeSetup()
    host.page_table = page_table
    host.operands = (q.reshape(n_hosted, SB_HEADS, SB_HEAD_DIM, 1), bias, kt_pages, vt_pages)
    host.in_specs = [seq_spec, _resident(bias.shape),
                     pl.BlockSpec(memory_space=pl.ANY), pl.BlockSpec(memory_space=pl.ANY)]
    host.out_spec = seq_spec
    host.out_shape = jax.ShapeDtypeStruct((n_hosted, SB_HEADS, SB_HEAD_DIM, 1), F32)
    host.scratch_shapes = [pltpu.VMEM((DECODE_RING, n) + page_shape, F32),
                           pltpu.VMEM((DECODE_RING, n) + page_shape, F32),
                           pltpu.SemaphoreType.DMA((2, DECODE_RING, n)),
                           pltpu.VMEM(page_shape, F32),
                           pltpu.VMEM((SB_HEADS, 1), F32)]
    return host


def _ffn_decode_kernel(pt_ref, h_ref, gpre_ref, wup_ref, wdown_ref, gpost_ref, *rest):
    o_ref = rest[HOSTED_INPUTS]
    side = _HostedDecode(pt_ref, rest[:HOSTED_INPUTS], rest[HOSTED_INPUTS + 1:], FFN_DECODE_PAGES)
    side.begin()
    _ffn_tile(h_ref, gpre_ref, wup_ref, wdown_ref, gpost_ref, o_ref, side)
    side.finish()


def _ffn_decode(h, g_pre, w_up_b, w_down_b, g_post, tm, decode_args):
    nb, seq, _ = h.shape
    row_spec = pl.BlockSpec((1, tm, D_MODEL), lambda b, i, pt: (b, i, 0))
    consts = (g_pre, w_up_b, w_down_b, g_post)
    host = _hosted_decode_setup(decode_args, FFN_DECODE_PAGES, nb, seq // tm)
    grid_spec = pltpu.PrefetchScalarGridSpec(
        num_scalar_prefetch=1,
        grid=(nb, seq // tm),
        in_specs=[row_spec] + [_resident(c.shape) for c in consts] + host.in_specs,
        out_specs=(row_spec, host.out_spec),
        scratch_shapes=host.scratch_shapes,
    )
    return pl.pallas_call(
        _ffn_decode_kernel,
        grid_spec=grid_spec,
        out_shape=(jax.ShapeDtypeStruct((nb, seq, D_MODEL), F32), host.out_shape),
        compiler_params=_params("arbitrary", "arbitrary"),
        name="ffn_decode",
    )(host.page_table, h, *consts, *host.operands)


def _proj_sample_kernel(x_ref, g_ref, win_ref, wconv_ref, s0_ref, s1_ref,
                        q_ref, k_ref, v_ref, yconv_ref, c_ref, xq_ref):
    xn = _rms(x_ref[...], g_ref[...]).astype(BF16)

    def proj(c):
        return _dot(xn, win_ref[:, c * IN_CHUNK:(c + 1) * IN_CHUNK])

    q_ref[...] = proj(0) * SB_QSCALE
    k_ref[...] = proj(1)
    v_ref[...] = proj(2)
    cb = proj(3)
    c = proj(4) * proj(5)
    w = wconv_ref[...]
    yconv_ref[...] = (cb * (s0_ref[...] * w[0:1] + s1_ref[...] * w[1:2] + c * w[2:3])).astype(BF16)
    c_ref[...] = c
    xq_ref[...] = proj(6) * XA_SCALE


def _proj_sample(x2d, g, w_in_b, w_conv, s0, s1):
    n = x2d.shape[0]
    wide = jax.ShapeDtypeStruct((n, IN_CHUNK), F32)
    return pl.pallas_call(
        _proj_sample_kernel,
        out_shape=(wide, wide, wide, jax.ShapeDtypeStruct((n, CONV_WIDTH), BF16), wide, wide),
        compiler_params=pltpu.CompilerParams(vmem_limit_bytes=VMEM_LIMIT),
        name="proj_sample",
    )(x2d, g, w_in_b, w_conv, s0, s1)


def _xattn_sample_kernel(xq_ref, mk_ref, mv_ref, o_ref):
    for s in range(xq_ref.shape[0]):
        xq = xq_ref[s]
        for h in range(XA_HEADS):
            sl = slice(h * XA_HEAD_DIM, (h + 1) * XA_HEAD_DIM)
            head_rows = pl.ds(h, N_MEM, stride=XA_HEADS)
            qh = jnp.broadcast_to(xq[:, sl], (8, XA_HEAD_DIM)).astype(BF16)
            sc = _dot_nt(qh, mk_ref[s, head_rows, :].astype(BF16))
            e = jnp.exp(sc - jnp.max(sc, axis=-1, keepdims=True))
            o = _dot(e.astype(BF16), mv_ref[s, head_rows, :].astype(BF16)) / jnp.sum(e, axis=-1, keepdims=True)
            o_ref[s, :, sl] = o[0:1].astype(BF16)


def _xattn_sample(xq, mk, mv):
    n = xq.shape[0]
    nblk = XA_SAMPLE_BLOCK
    q_spec = pl.BlockSpec((nblk, 1, XA_WIDTH), lambda b: (b, 0, 0))
    m_spec = pl.BlockSpec((nblk, N_MEM * XA_HEADS, XA_HEAD_DIM), lambda b: (b, 0, 0))
    return pl.pallas_call(
        _xattn_sample_kernel,
        grid=(n // nblk,),
        in_specs=[q_spec, m_spec, m_spec],
        out_specs=q_spec,
        out_shape=jax.ShapeDtypeStruct((n, 1, XA_WIDTH), BF16),
        compiler_params=_params("arbitrary"),
        name="xattn_sample",
    )(xq.reshape(n, 1, XA_WIDTH), mk, mv)


def _decode_weights(qb, bias, k_pages, carry):
    z = jnp.concatenate([jnp.sum(qb * kp, axis=1) for kp in k_pages], axis=0) + bias
    p_all = _softplus2(z)
    lane = lax.broadcasted_iota(jnp.int32, p_all.shape, 1)
    suffix = p_all
    shift = 1
    while shift < PAGE_SIZE:
        moved = pltpu.roll(suffix, PAGE_SIZE - shift, axis=1)
        suffix = suffix + jnp.where(lane < PAGE_SIZE - shift, moved, 0.0)
        shift *= 2
    ws = []
    for p in range(len(k_pages)):
        sl = slice(p * SB_HEADS, (p + 1) * SB_HEADS)
        ws.append(jnp.exp2((z[sl] - suffix[sl]) + carry))
        carry = carry - suffix[sl][:, 0:1]
    return ws, carry


def kernel(x_prompt, x_sample, mem_prompt, cache_k_pages, cache_v_pages, page_table, cache_mem_k,
           cache_mem_v, state_conv, g_mix_pre, w_in, b_sb, w_conv, g_mem, w_mem_kv, w_gate, b_gate,
           w_sb_o, w_conv_o, w_xa_o, w_o, g_mix_post, g_ffn_pre, w_up, w_down, g_ffn_post):
    depth = w_in.shape[0]
    assert depth == 1, "single-layer step"
    nb, seq, _ = x_prompt.shape
    nseq = x_sample.shape[0]
    bf = lambda a: a[0].astype(BF16)
    w_in_b, w_gate_b, w_mem_b = bf(w_in), bf(w_gate), bf(w_mem_kv)
    w_sb_b, w_cv_b, w_xa_b, w_o_b = bf(w_sb_o), bf(w_conv_o), bf(w_xa_o), bf(w_o)
    w_up_b, w_down_b = bf(w_up), bf(w_down)
    tail = (g_mix_pre, w_gate_b, b_gate, w_sb_b, w_cv_b, w_xa_b, w_o_b, g_mix_post)
    ffn_w = (g_ffn_pre, w_up_b, w_down_b, g_ffn_post)

    xs = x_sample.reshape(nseq, D_MODEL)
    q_s, k_s, v_s, yconv_s, c_s, xq_s = _proj_sample(
        xs, g_mix_pre, w_in_b, w_conv[0], state_conv[0, :, 0, :], state_conv[0, :, 1, :])
    kt_pages = jnp.transpose(cache_k_pages[0], (0, 2, 3, 1))
    vt_pages = jnp.transpose(cache_v_pages[0], (0, 2, 3, 1))

    kv, kv_b = _memkv(mem_prompt.reshape(nb * N_MEM, D_MODEL), g_mem, w_mem_b)
    mk = kv[:, :XA_WIDTH].reshape(1, nb, N_MEM, XA_HEADS, XA_HEAD_DIM)
    mv = kv[:, XA_WIDTH:].reshape(1, nb, N_MEM, XA_HEADS, XA_HEAD_DIM)
    mk_b = kv_b[:, :XA_WIDTH].reshape(nb, N_MEM, XA_WIDTH)
    mv_b = kv_b[:, XA_WIDTH:].reshape(nb, N_MEM, XA_WIDTH)
    q_p, k_p, v_p, kt, vt, yconv_p, cstate_p, yxa_p = _proj_prompt(
        x_prompt, g_mix_pre, w_in_b, w_conv[0], mk_b, mv_b, _bias_columns(b_sb[0]))
    ysb_p = _sb_prompt(q_p, k_p, v_p)
    h_p = _merge(x_prompt, ysb_p, yconv_p, yxa_p, *tail, tm=PROMPT_TILE)
    y_p, ysb_s = _ffn_decode(h_p, *ffn_w, PROMPT_TILE, (page_table, q_s, b_sb[0], kt_pages, vt_pages))
    to_cache = lambda a: jnp.transpose(
        a.reshape(nb, SB_HEADS, SB_HEAD_DIM, seq), (0, 3, 1, 2))[None]
    k_prompt, v_prompt = to_cache(kt), to_cache(vt)

    ysb_s = ysb_s.reshape(1, nseq, SB_WIDTH).astype(BF16)
    yxa_s = _xattn_sample(xq_s, cache_mem_k[0].reshape(nseq, N_MEM * XA_HEADS, XA_HEAD_DIM),
                          cache_mem_v[0].reshape(nseq, N_MEM * XA_HEADS, XA_HEAD_DIM))
    h_s = _merge(xs[None], ysb_s, yconv_s[None], yxa_s.reshape(1, nseq, XA_WIDTH), *tail, tm=nseq)
    y_s = _ffn(h_s, *ffn_w, tm=nseq)
    conv_sample = jnp.stack([state_conv[0, :, 1, :], c_s], axis=1)[None]
    to_tok = lambda a: a.reshape(1, nseq, 1, SB_HEADS, SB_HEAD_DIM)

    return (y_p, y_s.reshape(nseq, 1, D_MODEL), k_prompt, v_prompt, cstate_p[None], mk, mv,
            to_tok(k_s), to_tok(v_s), conv_sample)
```

```python
import math

import jax
import jax.numpy as jnp
from jax import lax
from jax.experimental import pallas as pl
from jax.experimental.pallas import tpu as pltpu

F32 = jnp.float32
BF16 = jnp.bfloat16

D_MODEL = 1024
SB_HEADS = 8
SB_HEAD_DIM = 64
SB_WIDTH = SB_HEADS * SB_HEAD_DIM
CONV_WIDTH = 512
CONV_K = 3
XA_HEADS = 4
XA_HEAD_DIM = 128
XA_WIDTH = XA_HEADS * XA_HEAD_DIM
N_MEM = 256
D_FF = 4 * D_MODEL
RMS_EPS = 1e-6
PAGE_SIZE = 128
IN_CHUNK = 512
LOG2E = math.log2(math.e)
SB_QSCALE = LOG2E / math.sqrt(SB_HEAD_DIM)
BIAS_TERMS = 3
MASKED_LOG2 = -1e30
XA_SCALE = 1.0 / math.sqrt(XA_HEAD_DIM)

V7X_VMEM_BYTES = 64 * 1024 * 1024
VMEM_LIMIT = V7X_VMEM_BYTES - 8 * 1024 * 1024

PROMPT_TILE = 512
SB_TILE = 256
SB_HEAD_GROUP = 8
FF_CHUNK = 256
HOST_CHUNKS = D_FF // FF_CHUNK
FFN_DECODE_PAGES = 8
XA_SAMPLE_BLOCK = 8
HOSTED_INPUTS = 4
DECODE_RING = 4


def _params(*sem):
    return pltpu.CompilerParams(dimension_semantics=sem, vmem_limit_bytes=VMEM_LIMIT)


def _resident(shape):
    nd = len(shape)
    return pl.BlockSpec(shape, lambda *_: (0,) * nd, pipeline_mode=pl.Buffered(1))


def _rms(x, g):
    return x * lax.rsqrt(jnp.mean(x * x, axis=-1, keepdims=True) + RMS_EPS) * g


def _dot(a, b):
    return jnp.dot(a, b, preferred_element_type=F32)


def _dot_nt(a, b):
    return lax.dot_general(a, b, (((1,), (1,)), ((), ())), preferred_element_type=F32)


def _softplus2(z2):
    return jnp.maximum(z2, jnp.log2(1.0 + jnp.exp2(jnp.minimum(z2, 64.0))))


def _bias_columns(b_sb):
    rest = b_sb.astype(F32) * LOG2E
    cols = []
    for _ in range(BIAS_TERMS):
        piece = rest.astype(BF16).astype(F32)
        cols.append(piece)
        rest = rest - piece
    pad = jnp.zeros((b_sb.shape[0], SB_HEAD_DIM - BIAS_TERMS), F32)
    return jnp.concatenate([jnp.stack(cols, axis=1), pad], axis=1)


def _neg_upper(n):
    s = lax.broadcasted_iota(jnp.int32, (n, n), 0)
    j = lax.broadcasted_iota(jnp.int32, (n, n), 1)
    return jnp.where(s > j, -1.0, 0.0).astype(BF16)


def _memkv_kernel(mem_ref, g_ref, w_ref, kv_ref, kvb_ref):
    mn = _rms(mem_ref[...], g_ref[...]).astype(BF16)
    kv = _dot(mn, w_ref[...])
    kv_ref[...] = kv
    kvb_ref[...] = kv.astype(BF16)


def _memkv(mem2d, g_mem, w_mem_kv_b):
    rows = mem2d.shape[0]
    return pl.pallas_call(
        _memkv_kernel,
        out_shape=(jax.ShapeDtypeStruct((rows, 2 * XA_WIDTH), F32),
                   jax.ShapeDtypeStruct((rows, 2 * XA_WIDTH), BF16)),
        compiler_params=pltpu.CompilerParams(vmem_limit_bytes=VMEM_LIMIT),
        name="memkv",
    )(mem2d, g_mem, w_mem_kv_b)


def _proj_prompt_kernel(x_ref, g_ref, win_ref, wconv_ref, mk_ref, mv_ref, kpad_ref,
                        q_ref, k_ref, v_ref, kt_ref, vt_ref, yconv_ref, cstate_ref, yxa_ref,
                        hist_ref):
    tm = x_ref.shape[1]
    i = pl.program_id(1)
    xn = _rms(x_ref[0], g_ref[...]).astype(BF16)

    def proj(c):
        return _dot(xn, win_ref[:, c * IN_CHUNK:(c + 1) * IN_CHUNK])

    q = proj(0) * SB_QSCALE
    k = proj(1)
    v = proj(2)
    lane = lax.broadcasted_iota(jnp.int32, (tm, SB_HEAD_DIM), 1)
    qpad = jnp.where(lane < BIAS_TERMS, 1.0, 0.0)
    for h in range(SB_HEADS):
        sl = slice(h * SB_HEAD_DIM, (h + 1) * SB_HEAD_DIM)
        kpad = jnp.broadcast_to(kpad_ref[h:h + 1, :], (tm, SB_HEAD_DIM))
        q_ref[0, h] = jnp.concatenate([q[:, sl], qpad], axis=1).astype(BF16)
        k_ref[0, h] = jnp.concatenate([k[:, sl], kpad], axis=1).astype(BF16)
        v_ref[0, h] = v[:, sl].astype(BF16)
    kt_ref[0] = k.T
    vt_ref[0] = v.T

    cb = proj(3)
    c = proj(4) * proj(5)

    @pl.when(i == 0)
    def _():
        hist_ref[0:8, :] = jnp.zeros((8, CONV_WIDTH), F32)

    hist_ref[8:8 + tm, :] = c
    c_m1 = hist_ref[pl.ds(7, tm), :]
    c_m2 = hist_ref[pl.ds(6, tm), :]
    w = wconv_ref[...]
    yconv_ref[0] = (cb * (c_m2 * w[0:1] + c_m1 * w[1:2] + c * w[2:3])).astype(BF16)
    tail = c[tm - (CONV_K - 1):tm, :]
    hist_ref[6:8, :] = tail
    cstate_ref[0] = tail

    xq = proj(6) * XA_SCALE
    for h in range(XA_HEADS):
        sl = slice(h * XA_HEAD_DIM, (h + 1) * XA_HEAD_DIM)
        s = _dot_nt(xq[:, sl].astype(BF16), mk_ref[0, :, sl])
        e = jnp.exp(s - jnp.max(s, axis=-1, keepdims=True))
        o = _dot(e.astype(BF16), mv_ref[0, :, sl]) / jnp.sum(e, axis=-1, keepdims=True)
        yxa_ref[0, :, sl] = o.astype(BF16)


def _proj_prompt(x, g, w_in_b, w_conv, mk_b, mv_b, kpad):
    nb, seq, _ = x.shape
    tm = PROMPT_TILE
    head_spec = pl.BlockSpec((1, SB_HEADS, tm, SB_HEAD_DIM), lambda b, i: (b, 0, i, 0))
    wide_spec = pl.BlockSpec((1, SB_HEADS, tm, 2 * SB_HEAD_DIM), lambda b, i: (b, 0, i, 0))
    t_spec = pl.BlockSpec((1, SB_WIDTH, tm), lambda b, i: (b, 0, i))
    row_spec = lambda w: pl.BlockSpec((1, tm, w), lambda b, i: (b, i, 0))
    mem_spec = pl.BlockSpec((1, N_MEM, XA_WIDTH), lambda b, i: (b, 0, 0))
    head_shape = jax.ShapeDtypeStruct((nb, SB_HEADS, seq, SB_HEAD_DIM), BF16)
    wide_shape = jax.ShapeDtypeStruct((nb, SB_HEADS, seq, 2 * SB_HEAD_DIM), BF16)
    t_shape = jax.ShapeDtypeStruct((nb, SB_WIDTH, seq), F32)
    return pl.pallas_call(
        _proj_prompt_kernel,
        grid=(nb, seq // tm),
        in_specs=[row_spec(D_MODEL), _resident(g.shape), _resident(w_in_b.shape),
                  _resident(w_conv.shape), mem_spec, mem_spec, _resident(kpad.shape)],
        out_specs=(wide_spec, wide_spec, head_spec, t_spec, t_spec, row_spec(CONV_WIDTH),
                   pl.BlockSpec((1, CONV_K - 1, CONV_WIDTH), lambda b, i: (b, 0, 0)),
                   row_spec(XA_WIDTH)),
        out_shape=(wide_shape, wide_shape, head_shape, t_shape, t_shape,
                   jax.ShapeDtypeStruct((nb, seq, CONV_WIDTH), BF16),
                   jax.ShapeDtypeStruct((nb, CONV_K - 1, CONV_WIDTH), F32),
                   jax.ShapeDtypeStruct((nb, seq, XA_WIDTH), BF16)),
        scratch_shapes=[pltpu.VMEM((tm + 8, CONV_WIDTH), F32)],
        compiler_params=_params("arbitrary", "arbitrary"),
        name="proj_prompt",
    )(x, g, w_in_b, w_conv, mk_b, mv_b, kpad)


def _sb_prompt_kernel(q_ref, k_ref, v_ref, negu_ref, o_ref, z_scr, p_scr, zmp_scr, w_scr, acc_scr,
                      carry_scr):
    ng, t = q_ref.shape[1], q_ref.shape[2]
    i = pl.program_id(2)
    negu = negu_ref[...]

    def rows(j):
        return pl.ds(pl.multiple_of(j * t, t), t)

    def logits(g, j):
        return _dot_nt(q_ref[0, g], k_ref[0, g, rows(j), :])

    def keep_and_beta(z):
        p = _softplus2(z)
        return p, z - p

    row = lax.broadcasted_iota(jnp.int32, (t, t), 0)
    col = lax.broadcasted_iota(jnp.int32, (t, t), 1)
    causal = col < row
    for g in range(ng):
        p, zmp = keep_and_beta(logits(g, i))
        p_scr[g] = jnp.where(causal, p, 0.0).astype(BF16)
        zmp_scr[g] = jnp.where(causal, zmp, MASKED_LOG2)
        z_scr[g] = logits(g, jnp.maximum(i - 1, 0))
        w_scr[g] = jnp.zeros((t, t), BF16)
        acc_scr[g] = jnp.zeros((t, SB_HEAD_DIM), F32)
        carry_scr[g] = jnp.zeros((t, 1), F32)

    def trip(n, _):
        j_prev = jnp.minimum(i - n + 1, i)
        j_next2 = jnp.maximum(i - n - 2, 0)
        for g in range(ng):
            acc_scr[g] += _dot(w_scr[g], v_ref[0, g, rows(j_prev), :])
            p_cur = p_scr[g]
            after = _dot(p_cur, negu)
            keep = after + carry_scr[g]
            w = jnp.exp2(zmp_scr[g] + keep)
            carry_scr[g] = keep[:, 0:1] - p_cur[:, 0:1].astype(F32)
            p, zmp = keep_and_beta(z_scr[g])
            z_scr[g] = logits(g, j_next2)
            w_scr[g] = w.astype(BF16)
            p_scr[g] = p.astype(BF16)
            zmp_scr[g] = zmp
        return 0

    lax.fori_loop(0, i + 1, trip, 0)
    heads = [acc_scr[g] + _dot(w_scr[g], v_ref[0, g, 0:t, :]) for g in range(ng)]
    o_ref[0] = jnp.concatenate(heads, axis=1).astype(BF16)


def _sb_prompt(q, k, v):
    nb, nh, seq, dh = v.shape
    t, ng = SB_TILE, SB_HEAD_GROUP
    kv_spec = lambda w: pl.BlockSpec((1, ng, seq, w), lambda b, h, i: (b, h, 0, 0),
                                     pipeline_mode=pl.Buffered(1))
    qo_spec = lambda w: pl.BlockSpec((1, ng, t, w), lambda b, h, i: (b, h, i, 0))
    return pl.pallas_call(
        _sb_prompt_kernel,
        grid=(nb, nh // ng, seq // t),
        in_specs=[qo_spec(q.shape[3]), kv_spec(k.shape[3]), kv_spec(dh), _resident((t, t))],
        out_specs=pl.BlockSpec((1, t, ng * dh), lambda b, h, i: (b, i, h)),
        out_shape=jax.ShapeDtypeStruct((nb, seq, nh * dh), BF16),
        scratch_shapes=[pltpu.VMEM((ng, t, t), F32),
                        pltpu.VMEM((ng, t, t), BF16), pltpu.VMEM((ng, t, t), F32),
                        pltpu.VMEM((ng, t, t), BF16), pltpu.VMEM((ng, t, dh), F32),
                        pltpu.VMEM((ng, t, 1), F32)],
        compiler_params=_params("arbitrary", "arbitrary", "arbitrary"),
        name="sb_prompt",
    )(q, k, v, _neg_upper(t))


def _merge_kernel(x_ref, ysb_ref, yconv_ref, yxa_ref, gpre_ref, wgate_ref, bgate_ref,
                  wsb_ref, wcv_ref, wxa_ref, wo_ref, gpost_ref, h_ref):
    x = x_ref[0]
    xn = _rms(x, gpre_ref[...]).astype(BF16)

    def gate(br):
        sl = slice(br * D_MODEL, (br + 1) * D_MODEL)
        return 1.0 / (1.0 + jnp.exp(-(_dot(xn, wgate_ref[:, sl]) + bgate_ref[:, sl])))

    m = gate(0) * _dot(ysb_ref[0], wsb_ref[...])
    m = m + gate(1) * _dot(yconv_ref[0], wcv_ref[...])
    m = m + gate(2) * _dot(yxa_ref[0], wxa_ref[...])
    mo = _dot(m.astype(BF16), wo_ref[...])
    h_ref[0] = x + _rms(mo, gpost_ref[...])


def _merge(x, ysb, yconv, yxa, g_pre, w_gate_b, b_gate, w_sb_b, w_cv_b, w_xa_b, w_o_b, g_post, tm):
    nb, seq, _ = x.shape
    row_spec = lambda w: pl.BlockSpec((1, tm, w), lambda b, i: (b, i, 0))
    consts = (g_pre, w_gate_b, b_gate, w_sb_b, w_cv_b, w_xa_b, w_o_b, g_post)
    return pl.pallas_call(
        _merge_kernel,
        grid=(nb, seq // tm),
        in_specs=[row_spec(D_MODEL), row_spec(SB_WIDTH), row_spec(CONV_WIDTH), row_spec(XA_WIDTH)]
                 + [_resident(c.shape) for c in consts],
        out_specs=row_spec(D_MODEL),
        out_shape=jax.ShapeDtypeStruct((nb, seq, D_MODEL), F32),
        compiler_params=_params("arbitrary", "arbitrary"),
        name="merge",
    )(x, ysb, yconv, yxa, *consts)


def _ffn_tile(h_ref, gpre_ref, wup_ref, wdown_ref, gpost_ref, o_ref, side):
    h = h_ref[0]
    hn = _rms(h, gpre_ref[...]).astype(BF16)
    f = jnp.zeros(h.shape, F32)
    for c in range(HOST_CHUNKS):
        sl = slice(c * FF_CHUNK, (c + 1) * FF_CHUNK)
        a = jnp.maximum(_dot(hn, wup_ref[:, sl]), 0.0)
        f = f + _dot((a * a).astype(BF16), wdown_ref[sl, :])
        side.chunk(c)
        side.advance(c)
    o_ref[0] = h + _rms(f, gpost_ref[...])


def _ffn_kernel(h_ref, gpre_ref, wup_ref, wdown_ref, gpost_ref, o_ref):
    _ffn_tile(h_ref, gpre_ref, wup_ref, wdown_ref, gpost_ref, o_ref, _NoSideWork())


def _ffn(h, g_pre, w_up_b, w_down_b, g_post, tm):
    nb, seq, _ = h.shape
    row_spec = pl.BlockSpec((1, tm, D_MODEL), lambda b, i: (b, i, 0))
    consts = (g_pre, w_up_b, w_down_b, g_post)
    return pl.pallas_call(
        _ffn_kernel,
        grid=(nb, seq // tm),
        in_specs=[row_spec] + [_resident(c.shape) for c in consts],
        out_specs=row_spec,
        out_shape=jax.ShapeDtypeStruct((nb, seq, D_MODEL), F32),
        compiler_params=_params("arbitrary", "arbitrary"),
        name="ffn",
    )(h, *consts)


class _NoSideWork:
    def chunk(self, c):
        pass

    def advance(self, c):
        pass


class _HostedDecode:
    def __init__(self, pt_ref, in_refs, out_and_scratch, n):
        self.pt = pt_ref
        self.q, self.bias, self.kt, self.vt = in_refs
        (self.ysb, self.kbuf, self.vbuf, self.sems, self.acc, self.carry,
         self.qb) = out_and_scratch
        self.n = n
        self.n_pages = pt_ref.shape[1]
        self.steps_per_seq = self.n_pages // (HOST_CHUNKS * n)
        self.step = pl.program_id(0) * pl.num_programs(1) + pl.program_id(1)
        self.n_steps = pl.num_programs(0) * pl.num_programs(1)
        self.part = self.step % self.steps_per_seq

    def _copies(self, steps_ahead, c):
        step = self.step + steps_ahead
        seq = step // self.steps_per_seq
        first = ((step % self.steps_per_seq) * HOST_CHUNKS + c) * self.n
        slot = c % DECODE_RING
        out = []
        for p in range(self.n):
            page = self.pt[seq, self.n_pages - 1 - (first + p)]
            out.append(pltpu.make_async_copy(self.kt.at[page], self.kbuf.at[slot, p],
                                             self.sems.at[0, slot, p]))
            out.append(pltpu.make_async_copy(self.vt.at[page], self.vbuf.at[slot, p],
                                             self.sems.at[1, slot, p]))
        return out

    def begin(self):
        @pl.when(self.step == 0)
        def _():
            for c in range(DECODE_RING):
                for cp in self._copies(0, c):
                    cp.start()

        @pl.when(self.step == 0)
        def _():
            self.ysb[...] = jnp.zeros(self.ysb.shape, F32)

        @pl.when(self.part == 0)
        def _():
            self.acc[...] = jnp.zeros(self.acc.shape, F32)
            self.carry[...] = jnp.zeros(self.carry.shape, F32)
            column = jnp.sum(self.q[...] * self._one_hot(), axis=1, keepdims=True)
            self.qb[...] = column.reshape(self.qb.shape)

        for cp in self._copies(0, 0):
            cp.wait()

    def _one_hot(self):
        lane = lax.broadcasted_iota(jnp.int32, (1, self.q.shape[1]), 1)
        return jnp.where(lane == self.step // self.steps_per_seq, 1.0, 0.0)

    def chunk(self, c):
        slot = c % DECODE_RING
        ws, carry = _decode_weights(self.qb[...], self.bias[...],
                                    [self.kbuf[slot, p] for p in range(self.n)], self.carry[...])
        self.carry[...] = carry
        for h in range(SB_HEADS):
            acc = self.acc[h]
            for p in range(self.n):
                acc = acc + ws[p][h:h + 1, :] * self.vbuf[slot, p, h]
            self.acc[h] = acc

    def advance(self, c):
        steps_ahead, c_ahead = divmod(c + DECODE_RING, HOST_CHUNKS)
        if steps_ahead == 0:
            for cp in self._copies(0, c_ahead):
                cp.start()
        else:
            @pl.when(self.step + steps_ahead < self.n_steps)
            def _():
                for cp in self._copies(steps_ahead, c_ahead):
                    cp.start()
        if c + 1 < HOST_CHUNKS:
            for cp in self._copies(0, c + 1):
                cp.wait()

    def finish(self):
        @pl.when(self.part == self.steps_per_seq - 1)
        def _():
            column = jnp.sum(self.acc[...], axis=2, keepdims=True).reshape(self.ysb.shape[0], 1)
            self.ysb[...] += column * self._one_hot()


class _HostedDecodeSetup:
    pass


def _hosted_decode_setup(decode_args, n, nb, steps_per_b):
    page_table, q, b_sb, kt_pages, vt_pages = decode_args
    n_pages = page_table.shape[1]
    pages_per_step = HOST_CHUNKS * n
    assert n_pages % pages_per_step == 0 and HOST_CHUNKS % DECODE_RING == 0
    steps_per_seq = n_pages // pages_per_step
    n_hosted = page_table.shape[0]
    assert n_hosted * steps_per_seq == nb * steps_per_b, "host grid must cover every decode sequence"
    page_shape = (SB_HEADS, SB_HEAD_DIM, PAGE_SIZE)
    bias = jnp.tile(b_sb.reshape(SB_HEADS, 1) * LOG2E, (n, 1))
    columns = (SB_WIDTH, n_hosted)
    host = _HostedDecodeSetup()
    host.page_table = page_table
    host.operands = (q.T, bias, kt_pages, vt_pages)
    host.in_specs = [_resident(columns), _resident(bias.shape),
                     pl.BlockSpec(memory_space=pl.ANY), pl.BlockSpec(memory_space=pl.ANY)]
    host.out_spec = pl.BlockSpec(columns, lambda b, i, pt: (0, 0))
    host.out_shape = jax.ShapeDtypeStruct(columns, F32)
    host.scratch_shapes = [pltpu.VMEM((DECODE_RING, n) + page_shape, F32),
                           pltpu.VMEM((DECODE_RING, n) + page_shape, F32),
                           pltpu.SemaphoreType.DMA((2, DECODE_RING, n)),
                           pltpu.VMEM(page_shape, F32),
                           pltpu.VMEM((SB_HEADS, 1), F32),
                           pltpu.VMEM((SB_HEADS, SB_HEAD_DIM, 1), F32)]
    return host


def _ffn_decode_kernel(pt_ref, h_ref, gpre_ref, wup_ref, wdown_ref, gpost_ref, *rest):
    o_ref = rest[HOSTED_INPUTS]
    side = _HostedDecode(pt_ref, rest[:HOSTED_INPUTS], rest[HOSTED_INPUTS + 1:], FFN_DECODE_PAGES)
    side.begin()
    _ffn_tile(h_ref, gpre_ref, wup_ref, wdown_ref, gpost_ref, o_ref, side)
    side.finish()


def _ffn_decode(h, g_pre, w_up_b, w_down_b, g_post, tm, decode_args):
    nb, seq, _ = h.shape
    row_spec = pl.BlockSpec((1, tm, D_MODEL), lambda b, i, pt: (b, i, 0))
    consts = (g_pre, w_up_b, w_down_b, g_post)
    host = _hosted_decode_setup(decode_args, FFN_DECODE_PAGES, nb, seq // tm)
    grid_spec = pltpu.PrefetchScalarGridSpec(
        num_scalar_prefetch=1,
        grid=(nb, seq // tm),
        in_specs=[row_spec] + [_resident(c.shape) for c in consts] + host.in_specs,
        out_specs=(row_spec, host.out_spec),
        scratch_shapes=host.scratch_shapes,
    )
    return pl.pallas_call(
        _ffn_decode_kernel,
        grid_spec=grid_spec,
        out_shape=(jax.ShapeDtypeStruct((nb, seq, D_MODEL), F32), host.out_shape),
        compiler_params=_params("arbitrary", "arbitrary"),
        name="ffn_decode",
    )(host.page_table, h, *consts, *host.operands)


def _proj_sample_kernel(x_ref, g_ref, win_ref, wconv_ref, s0_ref, s1_ref,
                        q_ref, k_ref, v_ref, yconv_ref, c_ref, xq_ref):
    xn = _rms(x_ref[...], g_ref[...]).astype(BF16)

    def proj(c):
        return _dot(xn, win_ref[:, c * IN_CHUNK:(c + 1) * IN_CHUNK])

    q_ref[...] = proj(0) * SB_QSCALE
    k_ref[...] = proj(1)
    v_ref[...] = proj(2)
    cb = proj(3)
    c = proj(4) * proj(5)
    w = wconv_ref[...]
    yconv_ref[...] = (cb * (s0_ref[...] * w[0:1] + s1_ref[...] * w[1:2] + c * w[2:3])).astype(BF16)
    c_ref[...] = c
    xq_ref[...] = proj(6) * XA_SCALE


def _proj_sample(x2d, g, w_in_b, w_conv, s0, s1):
    n = x2d.shape[0]
    wide = jax.ShapeDtypeStruct((n, IN_CHUNK), F32)
    return pl.pallas_call(
        _proj_sample_kernel,
        out_shape=(wide, wide, wide, jax.ShapeDtypeStruct((n, CONV_WIDTH), BF16), wide, wide),
        compiler_params=pltpu.CompilerParams(vmem_limit_bytes=VMEM_LIMIT),
        name="proj_sample",
    )(x2d, g, w_in_b, w_conv, s0, s1)


def _xattn_sample_kernel(xq_ref, mk_ref, mv_ref, o_ref):
    for s in range(xq_ref.shape[0]):
        xq = xq_ref[s]
        for h in range(XA_HEADS):
            sl = slice(h * XA_HEAD_DIM, (h + 1) * XA_HEAD_DIM)
            head_rows = pl.ds(h, N_MEM, stride=XA_HEADS)
            qh = jnp.broadcast_to(xq[:, sl], (8, XA_HEAD_DIM)).astype(BF16)
            sc = _dot_nt(qh, mk_ref[s, head_rows, :].astype(BF16))
            e = jnp.exp(sc - jnp.max(sc, axis=-1, keepdims=True))
            o = _dot(e.astype(BF16), mv_ref[s, head_rows, :].astype(BF16)) / jnp.sum(e, axis=-1, keepdims=True)
            o_ref[s, :, sl] = o[0:1].astype(BF16)


def _xattn_sample(xq, mk, mv):
    n = xq.shape[0]
    nblk = XA_SAMPLE_BLOCK
    q_spec = pl.BlockSpec((nblk, 1, XA_WIDTH), lambda b: (b, 0, 0))
    m_spec = pl.BlockSpec((nblk, N_MEM * XA_HEADS, XA_HEAD_DIM), lambda b: (b, 0, 0))
    return pl.pallas_call(
        _xattn_sample_kernel,
        grid=(n // nblk,),
        in_specs=[q_spec, m_spec, m_spec],
        out_specs=q_spec,
        out_shape=jax.ShapeDtypeStruct((n, 1, XA_WIDTH), BF16),
        compiler_params=_params("arbitrary"),
        name="xattn_sample",
    )(xq.reshape(n, 1, XA_WIDTH), mk, mv)


def _decode_weights(qb, bias, k_pages, carry):
    z = jnp.concatenate([jnp.sum(qb * kp, axis=1) for kp in k_pages], axis=0) + bias
    p_all = _softplus2(z)
    lane = lax.broadcasted_iota(jnp.int32, p_all.shape, 1)
    suffix = p_all
    shift = 1
    while shift < PAGE_SIZE:
        moved = pltpu.roll(suffix, PAGE_SIZE - shift, axis=1)
        suffix = suffix + jnp.where(lane < PAGE_SIZE - shift, moved, 0.0)
        shift *= 2
    ws = []
    for p in range(len(k_pages)):
        sl = slice(p * SB_HEADS, (p + 1) * SB_HEADS)
        ws.append(jnp.exp2((z[sl] - suffix[sl]) + carry))
        carry = carry - suffix[sl][:, 0:1]
    return ws, carry


def kernel(x_prompt, x_sample, mem_prompt, cache_k_pages, cache_v_pages, page_table, cache_mem_k,
           cache_mem_v, state_conv, g_mix_pre, w_in, b_sb, w_conv, g_mem, w_mem_kv, w_gate, b_gate,
           w_sb_o, w_conv_o, w_xa_o, w_o, g_mix_post, g_ffn_pre, w_up, w_down, g_ffn_post):
    depth = w_in.shape[0]
    assert depth == 1, "single-layer step"
    nb, seq, _ = x_prompt.shape
    nseq = x_sample.shape[0]
    bf = lambda a: a[0].astype(BF16)
    w_in_b, w_gate_b, w_mem_b = bf(w_in), bf(w_gate), bf(w_mem_kv)
    w_sb_b, w_cv_b, w_xa_b, w_o_b = bf(w_sb_o), bf(w_conv_o), bf(w_xa_o), bf(w_o)
    w_up_b, w_down_b = bf(w_up), bf(w_down)
    tail = (g_mix_pre, w_gate_b, b_gate, w_sb_b, w_cv_b, w_xa_b, w_o_b, g_mix_post)
    ffn_w = (g_ffn_pre, w_up_b, w_down_b, g_ffn_post)

    xs = x_sample.reshape(nseq, D_MODEL)
    q_s, k_s, v_s, yconv_s, c_s, xq_s = _proj_sample(
        xs, g_mix_pre, w_in_b, w_conv[0], state_conv[0, :, 0, :], state_conv[0, :, 1, :])
    kt_pages = jnp.transpose(cache_k_pages[0], (0, 2, 3, 1))
    vt_pages = jnp.transpose(cache_v_pages[0], (0, 2, 3, 1))

    kv, kv_b = _memkv(mem_prompt.reshape(nb * N_MEM, D_MODEL), g_mem, w_mem_b)
    mk = kv[:, :XA_WIDTH].reshape(1, nb, N_MEM, XA_HEADS, XA_HEAD_DIM)
    mv = kv[:, XA_WIDTH:].reshape(1, nb, N_MEM, XA_HEADS, XA_HEAD_DIM)
    mk_b = kv_b[:, :XA_WIDTH].reshape(nb, N_MEM, XA_WIDTH)
    mv_b = kv_b[:, XA_WIDTH:].reshape(nb, N_MEM, XA_WIDTH)
    q_p, k_p, v_p, kt, vt, yconv_p, cstate_p, yxa_p = _proj_prompt(
        x_prompt, g_mix_pre, w_in_b, w_conv[0], mk_b, mv_b, _bias_columns(b_sb[0]))
    ysb_p = _sb_prompt(q_p, k_p, v_p)
    h_p = _merge(x_prompt, ysb_p, yconv_p, yxa_p, *tail, tm=PROMPT_TILE)
    y_p, ysb_s = _ffn_decode(h_p, *ffn_w, PROMPT_TILE, (page_table, q_s, b_sb[0], kt_pages, vt_pages))
    to_cache = lambda a: jnp.transpose(
        a.reshape(nb, SB_HEADS, SB_HEAD_DIM, seq), (0, 3, 1, 2))[None]
    k_prompt, v_prompt = to_cache(kt), to_cache(vt)

    ysb_s = ysb_s.T.reshape(1, nseq, SB_WIDTH).astype(BF16)
    yxa_s = _xattn_sample(xq_s, cache_mem_k[0].reshape(nseq, N_MEM * XA_HEADS, XA_HEAD_DIM),
                          cache_mem_v[0].reshape(nseq, N_MEM * XA_HEADS, XA_HEAD_DIM))
    h_s = _merge(xs[None], ysb_s, yconv_s[None], yxa_s.reshape(1, nseq, XA_WIDTH), *tail, tm=nseq)
    y_s = _ffn(h_s, *ffn_w, tm=nseq)
    conv_sample = jnp.stack([state_conv[0, :, 1, :], c_s], axis=1)[None]
    to_tok = lambda a: a.reshape(1, nseq, 1, SB_HEADS, SB_HEAD_DIM)

    return (y_p, y_s.reshape(nseq, 1, D_MODEL), k_prompt, v_prompt, cstate_p[None], mk, mv,
            to_tok(k_s), to_tok(v_s), conv_sample)
```
